```python
import jax, jax.numpy as jnp
from jax import lax
import numpy as np

D_MODEL = 1024
BATCH = 4
SEQ = 4096
DEPTH = 4

CHUNK = 64
Q_BLOCK = 128
SC_WIDTH = 512
SC_GROUPS = 8
SC_KERNEL = 3
MLA_HEADS = 8
QK_NOPE = 64
QK_ROPE = 32
V_HEAD = 64
Q_LORA = 256
KV_LORA = 128
MLA_WIDTH = MLA_HEADS * V_HEAD
ROPE_THETA = 10000.0
CONF_WIDTH = D_MODEL
CONF_KERNEL = 31
EVEN_SPLITS = (SC_WIDTH, SC_WIDTH, SC_WIDTH, SC_WIDTH, Q_LORA, KV_LORA, QK_ROPE, MLA_WIDTH)
EVEN_IN = sum(EVEN_SPLITS)
ODD_IN = 3 * CONF_WIDTH
N_EVEN = (DEPTH + 1) // 2
N_ODD = DEPTH // 2
EPS = 1e-6

kernel_name = 'hybrid_chunk_causal_conv_mla_conformer_trunk'


def rms_norm(x, g):
    xf = x.astype(jnp.float32)
    y = xf * lax.rsqrt(jnp.mean(xf * xf, axis=-1, keepdims=True) + EPS)
    return (y * g.astype(jnp.float32)).astype(x.dtype)


def layer_norm(x, g, b):
    xf = x.astype(jnp.float32)
    mu = jnp.mean(xf, axis=-1, keepdims=True)
    var = jnp.mean(jnp.square(xf - mu), axis=-1, keepdims=True)
    y = (xf - mu) * lax.rsqrt(var + EPS)
    return (y * g.astype(jnp.float32) + b.astype(jnp.float32)).astype(x.dtype)


def split_cols(z, sizes):
    idx = np.cumsum(sizes)[:-1].tolist()
    return jnp.split(z, idx, axis=-1)


def causal_depthwise_conv(u, w, b):
    k, ch = w.shape
    y = lax.conv_general_dilated(u, w[:, None, :].astype(u.dtype), window_strides=(1,),
                                 padding=[(k - 1, 0)], dimension_numbers=('NWC', 'WIO', 'NWC'),
                                 feature_group_count=ch)
    return y + b.astype(u.dtype)


def rope_tables(positions):
    inv_freq = 1.0 / (ROPE_THETA ** (jnp.arange(0, QK_ROPE, 2, dtype=jnp.float32) / QK_ROPE))
    ang = positions.astype(jnp.float32)[..., None] * inv_freq
    return jnp.cos(ang), jnp.sin(ang)


def apply_rope(t, cos, sin):
    tf = t.astype(jnp.float32)
    t1, t2 = jnp.split(tf, 2, axis=-1)
    return jnp.concatenate([t1 * cos - t2 * sin, t2 * cos + t1 * sin], axis=-1).astype(t.dtype)


def block_causal_attention(q, k, v):
    b, h, s, dqk = q.shape
    nb = s // Q_BLOCK
    scale = 1.0 / np.sqrt(dqk)
    k_chunk = jnp.arange(s) // CHUNK
    qb = q.reshape(b, h, nb, Q_BLOCK, dqk).transpose(2, 0, 1, 3, 4)

    def one_block(args):
        q_blk, blk = args
        scores = jnp.einsum('bhqd,bhkd->bhqk', q_blk, k, preferred_element_type=jnp.float32) * scale
        q_chunk = (blk * Q_BLOCK + jnp.arange(Q_BLOCK)) // CHUNK
        allowed = k_chunk[None, :] <= q_chunk[:, None]
        scores = jnp.where(allowed, scores, jnp.finfo(jnp.float32).min)
        p = jax.nn.softmax(scores, axis=-1)
        return jnp.einsum('bhqk,bhkd->bhqd', p.astype(v.dtype), v)

    out = lax.map(one_block, (qb, jnp.arange(nb)))
    return out.transpose(1, 0, 3, 2, 4).reshape(b, s, h * v.shape[-1])


def even_mixer(h, cos, sin, w_in, sc_conv_w, sc_conv_b, q_norm_g, kv_norm_g, w_uq, w_ukv, w_out):
    bsz, s, _ = h.shape
    z = h @ w_in
    a_b, a_c, a_x, a_gate, c_q, c_kv, k_rope_raw, b_gate = split_cols(z, EVEN_SPLITS)
    y_a = a_b * causal_depthwise_conv(a_c * a_x, sc_conv_w, sc_conv_b)
    y_a = y_a * jax.nn.silu(a_gate)
    q = (rms_norm(c_q, q_norm_g) @ w_uq).reshape(bsz, s, MLA_HEADS, QK_NOPE + QK_ROPE)
    q_nope, q_rope = q[..., :QK_NOPE], q[..., QK_NOPE:]
    q_rope = apply_rope(q_rope, cos[:, :, None, :], sin[:, :, None, :])
    kv = (rms_norm(c_kv, kv_norm_g) @ w_ukv).reshape(bsz, s, MLA_HEADS, QK_NOPE + V_HEAD)
    k_nope, v = kv[..., :QK_NOPE], kv[..., QK_NOPE:]
    k_rope = apply_rope(k_rope_raw, cos, sin)
    k_rope = jnp.broadcast_to(k_rope[:, :, None, :], (bsz, s, MLA_HEADS, QK_ROPE))
    q_full = jnp.concatenate([q_nope, q_rope], axis=-1).transpose(0, 2, 1, 3)
    k_full = jnp.concatenate([k_nope, k_rope], axis=-1).transpose(0, 2, 1, 3)
    y_b = block_causal_attention(q_full, k_full, v.transpose(0, 2, 1, 3))
    y_b = y_b * jax.nn.silu(b_gate)
    return jnp.concatenate([y_a, y_b], axis=-1) @ w_out


def odd_mixer(h, w_in, conv_w, conv_b, ln_g, ln_b, w_out):
    z = h @ w_in
    val, glu_gate, silu_gate = jnp.split(z, 3, axis=-1)
    u = val * jax.nn.sigmoid(glu_gate)
    u = causal_depthwise_conv(u, conv_w, conv_b)
    u = jax.nn.silu(layer_norm(u, ln_g, ln_b))
    return (u * jax.nn.silu(silu_gate)) @ w_out


def setup_inputs(seed: int = 0) -> dict:
    key = jax.random.key(seed)
    ks = iter(jax.random.split(key, 32))

    def nrm(shape, scale):
        return jax.random.normal(next(ks), shape, jnp.float32) * scale

    d = D_MODEL
    return {
        'x': nrm((BATCH, SEQ, d), 1.0),
        'c': nrm((BATCH, d), 1.0),
        'positions': (jax.random.randint(next(ks), (BATCH, 1), 0, 4096, dtype=jnp.int32)
                      + jnp.arange(SEQ, dtype=jnp.int32)[None, :]),
        'ada_w': nrm((DEPTH, d, 3 * d), 0.5 * d ** -0.5),
        'ada_b': nrm((DEPTH, 3 * d), 0.1),
        'pre_norm_g': 1.0 + nrm((DEPTH, d), 0.05),
        'post_norm_g': 1.0 + nrm((DEPTH, d), 0.05),
        'even_w_in': nrm((N_EVEN, d, EVEN_IN), d ** -0.5),
        'even_sc_conv_w': nrm((N_EVEN, SC_KERNEL, SC_WIDTH), SC_KERNEL ** -0.5),
        'even_sc_conv_b': nrm((N_EVEN, SC_WIDTH), 0.01),
        'even_q_norm_g': 1.0 + nrm((N_EVEN, Q_LORA), 0.05),
        'even_kv_norm_g': 1.0 + nrm((N_EVEN, KV_LORA), 0.05),
        'even_w_uq': nrm((N_EVEN, Q_LORA, MLA_HEADS * (QK_NOPE + QK_ROPE)), Q_LORA ** -0.5),
        'even_w_ukv': nrm((N_EVEN, KV_LORA, MLA_HEADS * (QK_NOPE + V_HEAD)), KV_LORA ** -0.5),
        'even_w_out': nrm((N_EVEN, SC_WIDTH + MLA_WIDTH, d), (SC_WIDTH + MLA_WIDTH) ** -0.5),
        'odd_w_in': nrm((N_ODD, d, ODD_IN), d ** -0.5),
        'odd_conv_w': nrm((N_ODD, CONF_KERNEL, CONF_WIDTH), CONF_KERNEL ** -0.5),
        'odd_conv_b': nrm((N_ODD, CONF_WIDTH), 0.01),
        'odd_ln_g': 1.0 + nrm((N_ODD, CONF_WIDTH), 0.05),
        'odd_ln_b': nrm((N_ODD, CONF_WIDTH), 0.01),
        'odd_w_out': nrm((N_ODD, CONF_WIDTH, d), CONF_WIDTH ** -0.5),
    }


def reference(x, c, positions, ada_w, ada_b, pre_norm_g, post_norm_g,
              even_w_in, even_sc_conv_w, even_sc_conv_b, even_q_norm_g, even_kv_norm_g,
              even_w_uq, even_w_ukv, even_w_out,
              odd_w_in, odd_conv_w, odd_conv_b, odd_ln_g, odd_ln_b, odd_w_out):
    cos, sin = rope_tables(positions)
    c_act = jax.nn.silu(c)
    for layer in range(DEPTH):
        mod = c_act @ ada_w[layer] + ada_b[layer]
        shift, scale, gate = jnp.split(mod, 3, axis=-1)
        h = rms_norm(x, pre_norm_g[layer]) * (1.0 + scale[:, None, :]) + shift[:, None, :]
        i = layer // 2
        if layer % 2 == 0:
            y = even_mixer(h, cos, sin, even_w_in[i], even_sc_conv_w[i], even_sc_conv_b[i],
                           even_q_norm_g[i], even_kv_norm_g[i], even_w_uq[i], even_w_ukv[i],
                           even_w_out[i])
        else:
            y = odd_mixer(h, odd_w_in[i], odd_conv_w[i], odd_conv_b[i], odd_ln_g[i],
                          odd_ln_b[i], odd_w_out[i])
        x = x + gate[:, None, :] * rms_norm(y, post_norm_g[layer])
    return x
```

```python
import functools

import jax
import jax.numpy as jnp
from jax import lax
from jax.experimental import pallas as pl
from jax.experimental.pallas import tpu as pltpu

D_MODEL = 1024
DEPTH = 4
CHUNK = 64
SC_WIDTH = 512
SC_KERNEL = 3
MLA_HEADS = 8
QK_NOPE = 64
QK_ROPE = 32
V_HEAD = 64
Q_LORA = 256
KV_LORA = 128
MLA_WIDTH = MLA_HEADS * V_HEAD
ROPE_THETA = 10000.0
CONF_WIDTH = D_MODEL
CONF_KERNEL = 31
EPS = 1e-6

LANES = 128
HEAD_PAD = LANES
ROPE_LO = QK_NOPE
ROPE_HALF = QK_ROPE // 2
SUBLANES = 8
CONF_HALO = 32

OFF_AB, OFF_AC, OFF_AX, OFF_AG, OFF_BG = 0, 512, 1024, 1536, 2048
OFF_CQ = 2560
OFF_CKV = OFF_CQ + Q_LORA
OFF_KR = OFF_CKV + KV_LORA
EVEN_IN_PAD = OFF_KR + HEAD_PAD

TM = 256
TQ = 256
NEG_BIG = -1e30
VMEM_LIMIT = 56 * 1024 * 1024

F32 = jnp.float32
BF16 = jnp.bfloat16


def _rms(x, g):
    return x * lax.rsqrt(jnp.mean(x * x, axis=-1, keepdims=True) + EPS) * g


def _silu(x):
    return x * jax.nn.sigmoid(x)


def _swap_halves(t, is_t2):
    return jnp.where(is_t2, pltpu.roll(t, ROPE_HALF, 1), pltpu.roll(t, LANES - ROPE_HALF, 1))


def _mod_kernel(c_ref, w_ref, b_ref, o_ref):
    a = _silu(c_ref[...])
    o_ref[0] = jnp.dot(a.astype(BF16), w_ref[0].astype(BF16), preferred_element_type=F32) + b_ref[0]


def _mod_call(c_pad, ada_w, ada_b3):
    d = D_MODEL
    return pl.pallas_call(
        _mod_kernel,
        grid=(DEPTH, 3),
        in_specs=[
            pl.BlockSpec((SUBLANES, d), lambda l, j: (0, 0)),
            pl.BlockSpec((1, d, d), lambda l, j: (l, 0, j)),
            pl.BlockSpec((1, 1, d), lambda l, j: (l, 0, j)),
        ],
        out_specs=pl.BlockSpec((1, SUBLANES, d), lambda l, j: (l, 0, j)),
        out_shape=jax.ShapeDtypeStruct((DEPTH, SUBLANES, 3 * d), F32),
        compiler_params=pltpu.CompilerParams(dimension_semantics=("arbitrary", "arbitrary")),
    )(c_pad, ada_w, ada_b3)


def _rope_kernel(pos_ref, freq_ref, cos_ref, sin_ref):
    ang = pos_ref[0].astype(F32) * freq_ref[...]
    lane = lax.broadcasted_iota(jnp.int32, ang.shape, 1)
    is_t1 = (lane >= ROPE_LO) & (lane < ROPE_LO + ROPE_HALF)
    is_t2 = (lane >= ROPE_LO + ROPE_HALF) & (lane < ROPE_LO + QK_ROPE)
    c = jnp.cos(ang)
    s = jnp.sin(ang)
    cos_ref[0] = jnp.where(is_t1 | is_t2, c, 1.0)
    sin_ref[0] = jnp.where(is_t1, -s, jnp.where(is_t2, s, 0.0))


def _rope_call(pos3, freq_row):
    b, s, _ = pos3.shape
    ts = 512
    spec = pl.BlockSpec((1, ts, LANES), lambda i, j: (i, j, 0))
    return pl.pallas_call(
        _rope_kernel,
        grid=(b, s // ts),
        in_specs=[
            pl.BlockSpec((1, ts, 1), lambda i, j: (i, j, 0)),
            pl.BlockSpec((1, LANES), lambda i, j: (0, 0)),
        ],
        out_specs=[spec, spec],
        out_shape=[jax.ShapeDtypeStruct((b, s, LANES), F32)] * 2,
        compiler_params=pltpu.CompilerParams(dimension_semantics=("arbitrary", "arbitrary")),
    )(pos3, freq_row)


def _even_in_kernel(x_ref, shift_ref, scale_ref, png_ref, win_ref, scw_ref, scb_ref,
                    qg_ref, kvg_ref, wuq_ref, wuk_ref, wuv_ref, cos_ref, sin_ref,
                    ya_ref, bg_ref, q_ref, k_ref, v_ref, ubuf):
    tm = x_ref.shape[1]
    x = x_ref[0]
    h = _rms(x, png_ref[...]) * (1.0 + scale_ref[0]) + shift_ref[0]
    z = jnp.dot(h.astype(BF16), win_ref[...], preferred_element_type=F32)

    @pl.when(pl.program_id(1) == 0)
    def _():
        ubuf[0:SUBLANES, :] = jnp.zeros((SUBLANES, SC_WIDTH), F32)

    u = z[:, OFF_AC:OFF_AC + SC_WIDTH] * z[:, OFF_AX:OFF_AX + SC_WIDTH]
    ubuf[SUBLANES:SUBLANES + tm, :] = u
    conv = (scw_ref[2:3, :] * u
            + scw_ref[1:2, :] * ubuf[SUBLANES - 1:SUBLANES - 1 + tm, :]
            + scw_ref[0:1, :] * ubuf[SUBLANES - 2:SUBLANES - 2 + tm, :]
            + scb_ref[...])
    ubuf[0:SUBLANES, :] = ubuf[tm:tm + SUBLANES, :]
    y_a = z[:, OFF_AB:OFF_AB + SC_WIDTH] * conv * _silu(z[:, OFF_AG:OFF_AG + SC_WIDTH])
    ya_ref[0] = y_a.astype(BF16)
    bg_ref[0] = _silu(z[:, OFF_BG:OFF_BG + MLA_WIDTH]).astype(BF16)

    cos_t = cos_ref[0]
    sin_t = sin_ref[0]
    lane = lax.broadcasted_iota(jnp.int32, (tm, LANES), 1)
    is_t2 = lane >= ROPE_LO + ROPE_HALF

    def rope(t):
        return t * cos_t + _swap_halves(t, is_t2) * sin_t

    cq = _rms(z[:, OFF_CQ:OFF_CQ + Q_LORA], qg_ref[...]).astype(BF16)
    qf = jnp.dot(cq, wuq_ref[...], preferred_element_type=F32)
    ckv = _rms(z[:, OFF_CKV:OFF_CKV + KV_LORA], kvg_ref[...]).astype(BF16)
    kn = jnp.dot(ckv, wuk_ref[...], preferred_element_type=F32)
    v_ref[0] = jnp.dot(ckv, wuv_ref[...], preferred_element_type=F32).astype(BF16)
    kr = rope(z[:, OFF_KR:OFF_KR + HEAD_PAD])
    qk_scale = 1.0 / float(QK_NOPE + QK_ROPE) ** 0.5
    for hd in range(MLA_HEADS):
        sl = slice(hd * HEAD_PAD, (hd + 1) * HEAD_PAD)
        q_ref[0, :, sl] = (rope(qf[:, sl]) * qk_scale).astype(BF16)
        k_ref[0, :, sl] = (kn[:, sl] + kr).astype(BF16)


def _even_in_call(x, mod_l, png, win, scw, scb, qg, kvg, wuq, wuk, wuv, cos_t, sin_t):
    b, s, d = x.shape
    hp = MLA_HEADS * HEAD_PAD
    tile = lambda w: pl.BlockSpec((1, TM, w), lambda i, j: (i, j, 0))
    whole = lambda a: pl.BlockSpec(a.shape, lambda i, j: (0,) * a.ndim)
    modspec = lambda k: pl.BlockSpec((1, 1, d), lambda i, j: (i, 0, k))
    return pl.pallas_call(
        _even_in_kernel,
        grid=(b, s // TM),
        in_specs=[tile(d), modspec(0), modspec(1), whole(png), whole(win), whole(scw), whole(scb),
                  whole(qg), whole(kvg), whole(wuq), whole(wuk), whole(wuv), tile(LANES), tile(LANES)],
        out_specs=[tile(SC_WIDTH), tile(MLA_WIDTH), tile(hp), tile(hp), tile(hp)],
        out_shape=[jax.ShapeDtypeStruct((b, s, SC_WIDTH), BF16),
                   jax.ShapeDtypeStruct((b, s, MLA_WIDTH), BF16),
                   jax.ShapeDtypeStruct((b, s, hp), BF16),
                   jax.ShapeDtypeStruct((b, s, hp), BF16),
                   jax.ShapeDtypeStruct((b, s, hp), BF16)],
        scratch_shapes=[pltpu.VMEM((TM + SUBLANES, SC_WIDTH), F32)],
        compiler_params=pltpu.CompilerParams(dimension_semantics=("arbitrary", "arbitrary"),
                                             vmem_limit_bytes=VMEM_LIMIT),
    )(x, mod_l, mod_l, png, win, scw, scb, qg, kvg, wuq, wuk, wuv, cos_t, sin_t)


def _attn_kernel(q_ref, k_ref, v_ref, o_ref):
    qi = pl.program_id(1)
    tq = q_ref.shape[1]
    row = lax.broadcasted_iota(jnp.int32, (tq, tq), 0)
    col = lax.broadcasted_iota(jnp.int32, (tq, tq), 1)
    diag_ok = (col // CHUNK) <= (row // CHUNK)
    lane = lax.broadcasted_iota(jnp.int32, (tq, LANES), 1)

    def head(hd):
        sl = slice(hd * HEAD_PAD, (hd + 1) * HEAD_PAD)
        qh = q_ref[0, :, sl]

        def step(kb, carry, masked):
            m, l, acc = carry
            rows = pl.ds(pl.multiple_of(kb * tq, tq), tq)
            s = lax.dot_general(qh, k_ref[0, rows, sl], (((1,), (1,)), ((), ())),
                                preferred_element_type=F32)
            if masked:
                s = jnp.where(diag_ok, s, NEG_BIG)
            m_new = jnp.maximum(m, jnp.max(s, axis=-1, keepdims=True))
            alpha = jnp.exp(m - m_new)
            p = jnp.exp(s - m_new)
            l = alpha * l + jnp.sum(p, axis=-1, keepdims=True)
            acc = alpha * acc + jnp.dot(p.astype(BF16), v_ref[0, rows, sl],
                                        preferred_element_type=F32)
            return m_new, l, acc

        init = (jnp.full((tq, 1), NEG_BIG, F32), jnp.zeros((tq, 1), F32),
                jnp.zeros((tq, HEAD_PAD), F32))
        carry = lax.fori_loop(0, qi, functools.partial(step, masked=False), init)
        m, l, acc = step(qi, carry, True)
        return acc / l

    for pair in range(MLA_HEADS // 2):
        even = head(2 * pair)
        odd = head(2 * pair + 1)
        o_ref[0, :, pair * LANES:(pair + 1) * LANES] = jnp.where(
            lane < V_HEAD, even, pltpu.roll(odd, V_HEAD, 1)).astype(BF16)


def _attn_call(q, k, v):
    b, s, hp = q.shape
    kv_spec = pl.BlockSpec((1, s, hp), lambda i, j: (i, 0, 0))
    return pl.pallas_call(
        _attn_kernel,
        grid=(b, s // TQ),
        in_specs=[pl.BlockSpec((1, TQ, hp), lambda i, j: (i, j, 0)), kv_spec, kv_spec],
        out_specs=pl.BlockSpec((1, TQ, MLA_WIDTH), lambda i, j: (i, j, 0)),
        out_shape=jax.ShapeDtypeStruct((b, s, MLA_WIDTH), BF16),
        compiler_params=pltpu.CompilerParams(dimension_semantics=("arbitrary", "arbitrary"),
                                             vmem_limit_bytes=VMEM_LIMIT),
    )(q, k, v)


def _even_out_kernel(x_ref, ya_ref, yb_ref, bg_ref, gate_ref, pog_ref, wa_ref, wb_ref, o_ref):
    yb = (yb_ref[0].astype(F32) * bg_ref[0].astype(F32)).astype(BF16)
    y = (jnp.dot(ya_ref[0], wa_ref[...], preferred_element_type=F32)
         + jnp.dot(yb, wb_ref[...], preferred_element_type=F32))
    o_ref[0] = x_ref[0] + gate_ref[0] * _rms(y, pog_ref[...])


def _even_out_call(x, ya, yb, bg, mod_l, pog, wa, wb):
    b, s, d = x.shape
    tile = lambda w: pl.BlockSpec((1, TM, w), lambda i, j: (i, j, 0))
    whole = lambda a: pl.BlockSpec(a.shape, lambda i, j: (0,) * a.ndim)
    return pl.pallas_call(
        _even_out_kernel,
        grid=(b, s // TM),
        in_specs=[tile(d), tile(SC_WIDTH), tile(MLA_WIDTH), tile(MLA_WIDTH),
                  pl.BlockSpec((1, 1, d), lambda i, j: (i, 0, 2)), whole(pog), whole(wa), whole(wb)],
        out_specs=tile(d),
        out_shape=jax.ShapeDtypeStruct((b, s, d), F32),
        compiler_params=pltpu.CompilerParams(dimension_semantics=("arbitrary", "arbitrary"),
                                             vmem_limit_bytes=VMEM_LIMIT),
    )(x, ya, yb, bg, mod_l, pog, wa, wb)


def _odd_kernel(x_ref, shift_ref, scale_ref, gate_ref, png_ref, pog_ref, win_ref, cw_ref, cb_ref,
                lg_ref, lb_ref, wout_ref, o_ref, ubuf, vbuf):
    tm = x_ref.shape[1]
    cw = CONF_WIDTH
    x = x_ref[0]
    h = _rms(x, png_ref[...]) * (1.0 + scale_ref[0]) + shift_ref[0]
    z = jnp.dot(h.astype(BF16), win_ref[...], preferred_element_type=F32)

    @pl.when(pl.program_id(1) == 0)
    def _():
        ubuf[0:CONF_HALO, :] = jnp.zeros((CONF_HALO, cw), F32)

    ubuf[CONF_HALO:CONF_HALO + tm, :] = z[:, 0:cw] * jax.nn.sigmoid(z[:, cw:2 * cw])
    base = CONF_HALO - (CONF_KERNEL - 1)
    for cc in range(cw // LANES):
        cs = slice(cc * LANES, (cc + 1) * LANES)
        acc = jnp.broadcast_to(cb_ref[:, cs], (tm, LANES))
        for kk in range(CONF_KERNEL):
            acc = acc + cw_ref[kk:kk + 1, cs] * ubuf[base + kk:base + kk + tm, cs]
        vbuf[:, cs] = acc
    ubuf[0:CONF_HALO, :] = ubuf[tm:tm + CONF_HALO, :]

    v = vbuf[...]
    mu = jnp.mean(v, axis=-1, keepdims=True)
    vc = v - mu
    var = jnp.mean(vc * vc, axis=-1, keepdims=True)
    ln = vc * lax.rsqrt(var + EPS) * lg_ref[...] + lb_ref[...]
    t = _silu(ln) * _silu(z[:, 2 * cw:3 * cw])
    y = jnp.dot(t.astype(BF16), wout_ref[...], preferred_element_type=F32)
    o_ref[0] = x + gate_ref[0] * _rms(y, pog_ref[...])


def _odd_call(x, mod_l, png, pog, win, cw, cb, lg, lb, wout):
    b, s, d = x.shape
    tile = pl.BlockSpec((1, TM, d), lambda i, j: (i, j, 0))
    whole = lambda a: pl.BlockSpec(a.shape, lambda i, j: (0,) * a.ndim)
    modspec = lambda k: pl.BlockSpec((1, 1, d), lambda i, j: (i, 0, k))
    return pl.pallas_call(
        _odd_kernel,
        grid=(b, s // TM),
        in_specs=[tile, modspec(0), modspec(1), modspec(2), whole(png), whole(pog), whole(win),
                  whole(cw), whole(cb), whole(lg), whole(lb), whole(wout)],
        out_specs=tile,
        out_shape=jax.ShapeDtypeStruct((b, s, d), F32),
        scratch_shapes=[pltpu.VMEM((TM + CONF_HALO, CONF_WIDTH), F32),
                        pltpu.VMEM((TM, CONF_WIDTH), F32)],
        compiler_params=pltpu.CompilerParams(dimension_semantics=("arbitrary", "arbitrary"),
                                             vmem_limit_bytes=VMEM_LIMIT),
    )(x, mod_l, mod_l, mod_l, png, pog, win, cw, cb, lg, lb, wout)


def _pack_even_w_in(w):
    d = w.shape[0]
    a = w[:, :4 * SC_WIDTH]
    o = 4 * SC_WIDTH
    c_q = w[:, o:o + Q_LORA]
    c_kv = w[:, o + Q_LORA:o + Q_LORA + KV_LORA]
    k_r = w[:, o + Q_LORA + KV_LORA:o + Q_LORA + KV_LORA + QK_ROPE]
    b_g = w[:, o + Q_LORA + KV_LORA + QK_ROPE:]
    k_r = jnp.concatenate([jnp.zeros((d, ROPE_LO), w.dtype), k_r,
                           jnp.zeros((d, HEAD_PAD - ROPE_LO - QK_ROPE), w.dtype)], axis=1)
    return jnp.concatenate([a, b_g, c_q, c_kv, k_r], axis=1).astype(BF16)


def _pad_heads(w, width):
    k = w.shape[0]
    w = w.reshape(k, MLA_HEADS, width)
    w = jnp.pad(w, ((0, 0), (0, 0), (0, HEAD_PAD - width)))
    return w.reshape(k, MLA_HEADS * HEAD_PAD).astype(BF16)


def kernel(x, c, positions, ada_w, ada_b, pre_norm_g, post_norm_g, even_w_in, even_sc_conv_w, even_sc_conv_b, even_q_norm_g, even_kv_norm_g, even_w_uq, even_w_ukv, even_w_out, odd_w_in, odd_conv_w, odd_conv_b, odd_ln_g, odd_ln_b, odd_w_out):
    b, s, d = x.shape
    assert d == D_MODEL and s % TM == 0 and s % TQ == 0 and b <= SUBLANES
    assert TQ % CHUNK == 0

    c_pad = jnp.pad(c, ((0, SUBLANES - b), (0, 0)))
    mod = _mod_call(c_pad, ada_w, ada_b.reshape(DEPTH, 1, 3 * d))

    inv_freq = 1.0 / (ROPE_THETA ** (jnp.arange(0, QK_ROPE, 2, dtype=F32) / QK_ROPE))
    freq_row = jnp.concatenate([jnp.zeros((ROPE_LO,), F32), inv_freq, inv_freq,
                                jnp.zeros((HEAD_PAD - ROPE_LO - QK_ROPE,), F32)]).reshape(1, LANES)
    cos_t, sin_t = _rope_call(positions.reshape(b, s, 1), freq_row)

    row = lambda a: a.reshape(1, -1)
    for layer in range(DEPTH):
        mod_l = mod[layer, :b].reshape(b, 1, 3 * d)
        png, pog = row(pre_norm_g[layer]), row(post_norm_g[layer])
        i = layer // 2
        if layer % 2 == 0:
            w_ukv = even_w_ukv[i].reshape(KV_LORA, MLA_HEADS, QK_NOPE + V_HEAD)
            wuk = _pad_heads(w_ukv[:, :, :QK_NOPE].reshape(KV_LORA, -1), QK_NOPE)
            wuv = _pad_heads(w_ukv[:, :, QK_NOPE:].reshape(KV_LORA, -1), V_HEAD)
            wuq = _pad_heads(even_w_uq[i], QK_NOPE + QK_ROPE)
            ya, bg, q, k, v = _even_in_call(
                x, mod_l, png, _pack_even_w_in(even_w_in[i]), even_sc_conv_w[i],
                row(even_sc_conv_b[i]), row(even_q_norm_g[i]), row(even_kv_norm_g[i]),
                wuq, wuk, wuv, cos_t, sin_t)
            yb = _attn_call(q, k, v)
            w_out = even_w_out[i].astype(BF16)
            x = _even_out_call(x, ya, yb, bg, mod_l, pog, w_out[:SC_WIDTH], w_out[SC_WIDTH:])
        else:
            x = _odd_call(x, mod_l, png, pog, odd_w_in[i].astype(BF16), odd_conv_w[i],
                          row(odd_conv_b[i]), row(odd_ln_g[i]), row(odd_ln_b[i]),
                          odd_w_out[i].astype(BF16))
    return x
```

```python
import math

import jax
import jax.numpy as jnp
from jax import lax
from jax.experimental import pallas as pl
from jax.experimental.pallas import tpu as pltpu

D_MODEL = 1024
DEPTH = 4
CHUNK = 64
SC_WIDTH = 512
SC_KERNEL = 3
MLA_HEADS = 8
QK_NOPE = 64
QK_ROPE = 32
V_HEAD = 64
Q_LORA = 256
KV_LORA = 128
MLA_WIDTH = MLA_HEADS * V_HEAD
ROPE_THETA = 10000.0
CONF_WIDTH = D_MODEL
CONF_KERNEL = 31
EPS = 1e-6

LANES = 128
HEAD_PAD = LANES
ROPE_LO = QK_NOPE
ROPE_HALF = QK_ROPE // 2
SUBLANES = 8
CONF_HALO = 32

OFF_AB, OFF_AC, OFF_AX, OFF_AG, OFF_BG = 0, 512, 1024, 1536, 2048
OFF_CQ = 2560
OFF_CKV = OFF_CQ + Q_LORA
OFF_KR = OFF_CKV + KV_LORA
EVEN_IN_PAD = OFF_KR + HEAD_PAD

TM = 512
TQ = 256
TK = 256
CONV_ROWS = 128
QK_AHEAD = 8
NEG_BIG = -1e30
VMEM_LIMIT = 56 * 1024 * 1024

F32 = jnp.float32
BF16 = jnp.bfloat16


def _rms(x, g):
    return x * lax.rsqrt(jnp.mean(x * x, axis=-1, keepdims=True) + EPS) * g


def _silu(x):
    return x * jax.nn.sigmoid(x)


def _swap_halves(t, is_t2):
    return jnp.where(is_t2, pltpu.roll(t, ROPE_HALF, 1), pltpu.roll(t, LANES - ROPE_HALF, 1))


def _mod_kernel(c_ref, w_ref, b_ref, o_ref):
    a = _silu(c_ref[...])
    o_ref[0] = jnp.dot(a.astype(BF16), w_ref[0].astype(BF16), preferred_element_type=F32) + b_ref[0]


def _mod_call(c_pad, ada_w, ada_b3):
    d = D_MODEL
    return pl.pallas_call(
        _mod_kernel,
        grid=(DEPTH, 3),
        in_specs=[
            pl.BlockSpec((SUBLANES, d), lambda l, j: (0, 0)),
            pl.BlockSpec((1, d, d), lambda l, j: (l, 0, j)),
            pl.BlockSpec((1, 1, d), lambda l, j: (l, 0, j)),
        ],
        out_specs=pl.BlockSpec((1, SUBLANES, d), lambda l, j: (l, 0, j)),
        out_shape=jax.ShapeDtypeStruct((DEPTH, SUBLANES, 3 * d), F32),
        compiler_params=pltpu.CompilerParams(dimension_semantics=("arbitrary", "arbitrary")),
    )(c_pad, ada_w, ada_b3)


def _rope_kernel(pos_ref, freq_ref, cos_ref, sin_ref):
    ang = pos_ref[0].astype(F32) * freq_ref[...]
    lane = lax.broadcasted_iota(jnp.int32, ang.shape, 1)
    is_t1 = (lane >= ROPE_LO) & (lane < ROPE_LO + ROPE_HALF)
    is_t2 = (lane >= ROPE_LO + ROPE_HALF) & (lane < ROPE_LO + QK_ROPE)
    c = jnp.cos(ang)
    s = jnp.sin(ang)
    cos_ref[0] = jnp.where(is_t1 | is_t2, c, 1.0)
    sin_ref[0] = jnp.where(is_t1, -s, jnp.where(is_t2, s, 0.0))


def _rope_call(pos3, freq_row):
    b, s, _ = pos3.shape
    ts = 512
    spec = pl.BlockSpec((1, ts, LANES), lambda i, j: (i, j, 0))
    return pl.pallas_call(
        _rope_kernel,
        grid=(b, s // ts),
        in_specs=[
            pl.BlockSpec((1, ts, 1), lambda i, j: (i, j, 0)),
            pl.BlockSpec((1, LANES), lambda i, j: (0, 0)),
        ],
        out_specs=[spec, spec],
        out_shape=[jax.ShapeDtypeStruct((b, s, LANES), F32)] * 2,
        compiler_params=pltpu.CompilerParams(dimension_semantics=("arbitrary", "arbitrary")),
    )(pos3, freq_row)


def _even_in_kernel(x_ref, shift_ref, scale_ref, png_ref, win_ref, scw_ref, scb_ref,
                    qg_ref, kvg_ref, wuq_ref, wuk_ref, wuvt_ref, cos_ref, sin_ref,
                    ya_ref, bg_ref, q_ref, k_ref, vt_ref, ubuf):
    tm = x_ref.shape[1]
    x = x_ref[0]
    h = _rms(x, png_ref[...]) * (1.0 + scale_ref[0]) + shift_ref[0]
    z = jnp.dot(h.astype(BF16), win_ref[...], preferred_element_type=F32)

    @pl.when(pl.program_id(1) == 0)
    def _():
        ubuf[0:SUBLANES, :] = jnp.zeros((SUBLANES, SC_WIDTH), F32)

    u = z[:, OFF_AC:OFF_AC + SC_WIDTH] * z[:, OFF_AX:OFF_AX + SC_WIDTH]
    ubuf[SUBLANES:SUBLANES + tm, :] = u
    conv = (scw_ref[2:3, :] * u
            + scw_ref[1:2, :] * ubuf[SUBLANES - 1:SUBLANES - 1 + tm, :]
            + scw_ref[0:1, :] * ubuf[SUBLANES - 2:SUBLANES - 2 + tm, :]
            + scb_ref[...])
    ubuf[0:SUBLANES, :] = ubuf[tm:tm + SUBLANES, :]
    y_a = z[:, OFF_AB:OFF_AB + SC_WIDTH] * conv * _silu(z[:, OFF_AG:OFF_AG + SC_WIDTH])
    ya_ref[0] = y_a.astype(BF16)
    bg_ref[0] = _silu(z[:, OFF_BG:OFF_BG + MLA_WIDTH]).astype(BF16)

    cos_t = cos_ref[0]
    sin_t = sin_ref[0]
    lane = lax.broadcasted_iota(jnp.int32, (tm, LANES), 1)
    is_t2 = lane >= ROPE_LO + ROPE_HALF

    def rope(t):
        return t * cos_t + _swap_halves(t, is_t2) * sin_t

    cq = _rms(z[:, OFF_CQ:OFF_CQ + Q_LORA], qg_ref[...]).astype(BF16)
    qf = jnp.dot(cq, wuq_ref[...], preferred_element_type=F32)
    ckv = _rms(z[:, OFF_CKV:OFF_CKV + KV_LORA], kvg_ref[...]).astype(BF16)
    kn = jnp.dot(ckv, wuk_ref[...], preferred_element_type=F32)
    vt = lax.dot_general(wuvt_ref[...], ckv, (((1,), (1,)), ((), ())),
                         preferred_element_type=F32).astype(BF16)
    for j in range(tm // TK):
        vt_ref[0, j] = vt[:, j * TK:(j + 1) * TK]
    kr = rope(z[:, OFF_KR:OFF_KR + HEAD_PAD])
    q_scale = math.log2(math.e) / math.sqrt(QK_NOPE + QK_ROPE)
    for hd in range(MLA_HEADS):
        sl = slice(hd * HEAD_PAD, (hd + 1) * HEAD_PAD)
        q_ref[0, :, sl] = (rope(qf[:, sl]) * q_scale).astype(BF16)
        k_ref[0, :, sl] = (kn[:, sl] + kr).astype(BF16)


def _even_in_call(x, mod_l, png, win, scw, scb, qg, kvg, wuq, wuk, wuvt, cos_t, sin_t):
    b, s, d = x.shape
    hp = MLA_HEADS * HEAD_PAD
    tile = lambda w: pl.BlockSpec((1, TM, w), lambda i, j: (i, j, 0))
    whole = lambda a: pl.BlockSpec(a.shape, lambda i, j: (0,) * a.ndim)
    modspec = lambda k: pl.BlockSpec((1, 1, d), lambda i, j: (i, 0, k))
    return pl.pallas_call(
        _even_in_kernel,
        grid=(b, s // TM),
        in_specs=[tile(d), modspec(0), modspec(1), whole(png), whole(win), whole(scw), whole(scb),
                  whole(qg), whole(kvg), whole(wuq), whole(wuk), whole(wuvt), tile(LANES), tile(LANES)],
        out_specs=[tile(SC_WIDTH), tile(MLA_WIDTH), tile(hp), tile(hp),
                   pl.BlockSpec((1, TM // TK, MLA_WIDTH, TK), lambda i, j: (i, j, 0, 0))],
        out_shape=[jax.ShapeDtypeStruct((b, s, SC_WIDTH), BF16),
                   jax.ShapeDtypeStruct((b, s, MLA_WIDTH), BF16),
                   jax.ShapeDtypeStruct((b, s, hp), BF16),
                   jax.ShapeDtypeStruct((b, s, hp), BF16),
                   jax.ShapeDtypeStruct((b, s // TK, MLA_WIDTH, TK), BF16)],
        scratch_shapes=[pltpu.VMEM((TM + SUBLANES, SC_WIDTH), F32)],
        compiler_params=pltpu.CompilerParams(dimension_semantics=("arbitrary", "arbitrary"),
                                             vmem_limit_bytes=VMEM_LIMIT),
    )(x, mod_l, mod_l, png, win, scw, scb, qg, kvg, wuq, wuk, wuvt, cos_t, sin_t)


def _attn_out_kernel(q_ref, k_ref, vt_ref, bg_ref, x_ref, ya_ref, gate_ref, pog_ref, wa_ref, wb_ref,
                     o_ref, m_s, l_s, acc_s, yb_s):
    qi = pl.program_id(1)
    m_s[...] = jnp.full(m_s.shape, NEG_BIG, F32)
    l_s[...] = jnp.zeros(l_s.shape, F32)
    acc_s[...] = jnp.zeros(acc_s.shape, F32)

    key = lax.broadcasted_iota(jnp.int32, (TK, TQ), 0)
    qry = lax.broadcasted_iota(jnp.int32, (TK, TQ), 1)
    diag_ok = (key // CHUNK) <= (qry // CHUNK)

    def block(kb, masked):
        rows = pl.ds(pl.multiple_of(kb * TK, TK), TK)

        def scores(hd):
            sl = slice(hd * HEAD_PAD, (hd + 1) * HEAD_PAD)
            return lax.dot_general(k_ref[0, rows, sl], q_ref[0, :, sl], (((1,), (1,)), ((), ())),
                                   preferred_element_type=F32)

        def update(hd, s):
            if masked:
                s = jnp.where(diag_ok, s, NEG_BIG)
            m_prev = m_s[hd]
            m_new = jnp.maximum(m_prev, jnp.max(s, axis=0, keepdims=True))
            alpha = jnp.exp2(m_prev - m_new)
            p = jnp.exp2(s - m_new)
            l_s[hd] = alpha * l_s[hd] + jnp.sum(p, axis=0, keepdims=True)
            pv = jnp.dot(vt_ref[0, kb, hd * V_HEAD:(hd + 1) * V_HEAD, :], p.astype(BF16),
                         preferred_element_type=F32)
            acc_s[hd] = alpha * acc_s[hd] + pv
            m_s[hd] = m_new

        pending = {hd: scores(hd) for hd in range(QK_AHEAD)}
        for hd in range(MLA_HEADS):
            if hd + QK_AHEAD < MLA_HEADS:
                pending[hd + QK_AHEAD] = scores(hd + QK_AHEAD)
            update(hd, pending.pop(hd))

    def body(kb, carry):
        block(kb, False)
        return carry

    lax.fori_loop(0, qi, body, 0)
    block(qi, True)

    for pair in range(MLA_HEADS // 2):
        both = jnp.concatenate(
            [acc_s[2 * pair] * (1.0 / l_s[2 * pair]), acc_s[2 * pair + 1] * (1.0 / l_s[2 * pair + 1])],
            axis=0)
        cols = slice(pair * LANES, (pair + 1) * LANES)
        yb_s[:, cols] = (both.T * bg_ref[0, :, cols].astype(F32)).astype(BF16)
    y = (jnp.dot(ya_ref[0], wa_ref[...], preferred_element_type=F32)
         + jnp.dot(yb_s[...], wb_ref[...], preferred_element_type=F32))
    o_ref[0] = x_ref[0] + gate_ref[0] * _rms(y, pog_ref[...])


def _attn_out_call(q, k, vt, bg, x, ya, mod_l, pog, wa, wb):
    b, s, d = x.shape
    hp = MLA_HEADS * HEAD_PAD
    tile = lambda w: pl.BlockSpec((1, TQ, w), lambda i, j: (i, j, 0))
    whole = lambda a: pl.BlockSpec(a.shape, lambda i, j: (0,) * a.ndim)
    return pl.pallas_call(
        _attn_out_kernel,
        grid=(b, s // TQ),
        in_specs=[tile(hp),
                  pl.BlockSpec((1, s, hp), lambda i, j: (i, 0, 0)),
                  pl.BlockSpec((1, s // TK, MLA_WIDTH, TK), lambda i, j: (i, 0, 0, 0)),
                  tile(MLA_WIDTH), tile(d), tile(SC_WIDTH),
                  pl.BlockSpec((1, 1, d), lambda i, j: (i, 0, 2)), whole(pog), whole(wa), whole(wb)],
        out_specs=tile(d),
        out_shape=jax.ShapeDtypeStruct((b, s, d), F32),
        scratch_shapes=[pltpu.VMEM((MLA_HEADS, 1, TQ), F32),
                        pltpu.VMEM((MLA_HEADS, 1, TQ), F32),
                        pltpu.VMEM((MLA_HEADS, V_HEAD, TQ), F32),
                        pltpu.VMEM((TQ, MLA_WIDTH), BF16)],
        compiler_params=pltpu.CompilerParams(dimension_semantics=("arbitrary", "arbitrary"),
                                             vmem_limit_bytes=VMEM_LIMIT),
    )(q, k, vt, bg, x, ya, mod_l, pog, wa, wb)


def _odd_kernel(x_ref, shift_ref, scale_ref, gate_ref, png_ref, pog_ref, win_ref, cw_ref, cb_ref,
                lg_ref, lb_ref, wout_ref, o_ref, ubuf, vbuf):
    tm = x_ref.shape[1]
    cw = CONF_WIDTH
    x = x_ref[0]
    h = _rms(x, png_ref[...]) * (1.0 + scale_ref[0]) + shift_ref[0]
    z = jnp.dot(h.astype(BF16), win_ref[...], preferred_element_type=F32)

    @pl.when(pl.program_id(1) == 0)
    def _():
        ubuf[0:CONF_HALO, :] = jnp.zeros((CONF_HALO, cw), F32)

    ubuf[CONF_HALO:CONF_HALO + tm, :] = z[:, 0:cw] * jax.nn.sigmoid(z[:, cw:2 * cw])

    for cc in range(cw // LANES):
        cs = slice(cc * LANES, (cc + 1) * LANES)
        for r0 in range(0, tm, CONV_ROWS):
            acc = jnp.broadcast_to(cb_ref[:, cs], (CONV_ROWS, LANES))
            window = ubuf[r0:r0 + CONV_ROWS + CONF_HALO, cs]
            for r in range(SUBLANES):
                shifted = window if r == 0 else pltpu.roll(window, r, 0)
                for a in range(CONF_HALO // SUBLANES):
                    j = SUBLANES * a + r
                    if j >= CONF_KERNEL:
                        continue
                    kk = CONF_KERNEL - 1 - j
                    start = CONF_HALO - SUBLANES * a
                    acc = acc + cw_ref[kk:kk + 1, cs] * shifted[start:start + CONV_ROWS]
            vbuf[r0:r0 + CONV_ROWS, cs] = acc
    ubuf[0:CONF_HALO, :] = ubuf[tm:tm + CONF_HALO, :]

    v = vbuf[...]
    mu = jnp.mean(v, axis=-1, keepdims=True)
    vc = v - mu
    var = jnp.mean(vc * vc, axis=-1, keepdims=True)
    ln = vc * lax.rsqrt(var + EPS) * lg_ref[...] + lb_ref[...]
    t = _silu(ln) * _silu(z[:, 2 * cw:3 * cw])
    y = jnp.dot(t.astype(BF16), wout_ref[...], preferred_element_type=F32)
    o_ref[0] = x + gate_ref[0] * _rms(y, pog_ref[...])


def _odd_call(x, mod_l, png, pog, win, cw, cb, lg, lb, wout):
    b, s, d = x.shape
    tile = pl.BlockSpec((1, TM, d), lambda i, j: (i, j, 0))
    whole = lambda a: pl.BlockSpec(a.shape, lambda i, j: (0,) * a.ndim)
    modspec = lambda k: pl.BlockSpec((1, 1, d), lambda i, j: (i, 0, k))
    return pl.pallas_call(
        _odd_kernel,
        grid=(b, s // TM),
        in_specs=[tile, modspec(0), modspec(1), modspec(2), whole(png), whole(pog), whole(win),
                  whole(cw), whole(cb), whole(lg), whole(lb), whole(wout)],
        out_specs=tile,
        out_shape=jax.ShapeDtypeStruct((b, s, d), F32),
        scratch_shapes=[pltpu.VMEM((TM + CONF_HALO, CONF_WIDTH), F32),
                        pltpu.VMEM((TM, CONF_WIDTH), F32)],
        compiler_params=pltpu.CompilerParams(dimension_semantics=("arbitrary", "arbitrary"),
                                             vmem_limit_bytes=VMEM_LIMIT),
    )(x, mod_l, mod_l, mod_l, png, pog, win, cw, cb, lg, lb, wout)


def _pack_even_w_in(w):
    d = w.shape[0]
    a = w[:, :4 * SC_WIDTH]
    o = 4 * SC_WIDTH
    c_q = w[:, o:o + Q_LORA]
    c_kv = w[:, o + Q_LORA:o + Q_LORA + KV_LORA]
    k_r = w[:, o + Q_LORA + KV_LORA:o + Q_LORA + KV_LORA + QK_ROPE]
    b_g = w[:, o + Q_LORA + KV_LORA + QK_ROPE:]
    k_r = jnp.concatenate([jnp.zeros((d, ROPE_LO), w.dtype), k_r,
                           jnp.zeros((d, HEAD_PAD - ROPE_LO - QK_ROPE), w.dtype)], axis=1)
    return jnp.concatenate([a, b_g, c_q, c_kv, k_r], axis=1).astype(BF16)


def _pad_heads(w, width):
    k = w.shape[0]
    w = w.reshape(k, MLA_HEADS, width)
    w = jnp.pad(w, ((0, 0), (0, 0), (0, HEAD_PAD - width)))
    return w.reshape(k, MLA_HEADS * HEAD_PAD).astype(BF16)


def kernel(x, c, positions, ada_w, ada_b, pre_norm_g, post_norm_g, even_w_in, even_sc_conv_w, even_sc_conv_b, even_q_norm_g, even_kv_norm_g, even_w_uq, even_w_ukv, even_w_out, odd_w_in, odd_conv_w, odd_conv_b, odd_ln_g, odd_ln_b, odd_w_out):
    b, s, d = x.shape
    assert d == D_MODEL and s % TM == 0 and s % TQ == 0 and TM % TK == 0 and b <= SUBLANES
    assert TQ % CHUNK == 0 and TK % CHUNK == 0 and TQ == TK and TM % CONV_ROWS == 0

    c_pad = jnp.pad(c, ((0, SUBLANES - b), (0, 0)))
    mod = _mod_call(c_pad, ada_w, ada_b.reshape(DEPTH, 1, 3 * d))

    inv_freq = 1.0 / (ROPE_THETA ** (jnp.arange(0, QK_ROPE, 2, dtype=F32) / QK_ROPE))
    freq_row = jnp.concatenate([jnp.zeros((ROPE_LO,), F32), inv_freq, inv_freq,
                                jnp.zeros((HEAD_PAD - ROPE_LO - QK_ROPE,), F32)]).reshape(1, LANES)
    cos_t, sin_t = _rope_call(positions.reshape(b, s, 1), freq_row)

    row = lambda a: a.reshape(1, -1)
    for layer in range(DEPTH):
        mod_l = mod[layer, :b].reshape(b, 1, 3 * d)
        png, pog = row(pre_norm_g[layer]), row(post_norm_g[layer])
        i = layer // 2
        if layer % 2 == 0:
            w_ukv = even_w_ukv[i].reshape(KV_LORA, MLA_HEADS, QK_NOPE + V_HEAD)
            wuk = _pad_heads(w_ukv[:, :, :QK_NOPE].reshape(KV_LORA, -1), QK_NOPE)
            wuvt = w_ukv[:, :, QK_NOPE:].reshape(KV_LORA, MLA_WIDTH).T.astype(BF16)
            wuq = _pad_heads(even_w_uq[i], QK_NOPE + QK_ROPE)
            ya, bg, q, k, vt = _even_in_call(
                x, mod_l, png, _pack_even_w_in(even_w_in[i]), even_sc_conv_w[i],
                row(even_sc_conv_b[i]), row(even_q_norm_g[i]), row(even_kv_norm_g[i]),
                wuq, wuk, wuvt, cos_t, sin_t)
            w_out = even_w_out[i].astype(BF16)
            x = _attn_out_call(q, k, vt, bg, x, ya, mod_l, pog, w_out[:SC_WIDTH], w_out[SC_WIDTH:])
        else:
            x = _odd_call(x, mod_l, png, pog, odd_w_in[i].astype(BF16), odd_conv_w[i],
                          row(odd_conv_b[i]), row(odd_ln_g[i]), row(odd_ln_b[i]),
                          odd_w_out[i].astype(BF16))
    return x
```

```python
import math

import jax
import jax.numpy as jnp
from jax import lax
from jax.experimental import pallas as pl
from jax.experimental.pallas import tpu as pltpu

D_MODEL = 1024
DEPTH = 4
CHUNK = 64
SC_WIDTH = 512
SC_KERNEL = 3
MLA_HEADS = 8
QK_NOPE = 64
QK_ROPE = 32
V_HEAD = 64
Q_LORA = 256
KV_LORA = 128
MLA_WIDTH = MLA_HEADS * V_HEAD
ROPE_THETA = 10000.0
CONF_WIDTH = D_MODEL
CONF_KERNEL = 31
EPS = 1e-6

LANES = 128
HEAD_PAD = LANES
ROPE_LO = QK_NOPE
ROPE_HALF = QK_ROPE // 2
SUBLANES = 8
CONF_HALO = 32

OFF_AB, OFF_AC, OFF_AX, OFF_AG, OFF_BG = 0, 512, 1024, 1536, 2048
OFF_CQ = 2560
OFF_CKV = OFF_CQ + Q_LORA
OFF_KR = OFF_CKV + KV_LORA
EVEN_IN_PAD = OFF_KR + HEAD_PAD

TM = 512
CONF_GROUP = 256
TQ = 256
TK = 256
CONV_ROWS = 128
QK_AHEAD = 4
NEG_BIG = -1e30
VMEM_LIMIT = 56 * 1024 * 1024

F32 = jnp.float32
BF16 = jnp.bfloat16


def _rms(x, g):
    return x * lax.rsqrt(jnp.mean(x * x, axis=-1, keepdims=True) + EPS) * g


def _silu(x):
    return x * jax.nn.sigmoid(x)


def _swap_halves(t, is_t2):
    return jnp.where(is_t2, pltpu.roll(t, ROPE_HALF, 1), pltpu.roll(t, LANES - ROPE_HALF, 1))


def _mod_kernel(c_ref, w_ref, b_ref, o_ref):
    a = _silu(c_ref[...])
    o_ref[0] = jnp.dot(a.astype(BF16), w_ref[0].astype(BF16), preferred_element_type=F32) + b_ref[0]


def _mod_call(c_pad, ada_w, ada_b3):
    d = D_MODEL
    return pl.pallas_call(
        _mod_kernel,
        grid=(DEPTH, 3),
        in_specs=[
            pl.BlockSpec((SUBLANES, d), lambda l, j: (0, 0)),
            pl.BlockSpec((1, d, d), lambda l, j: (l, 0, j)),
            pl.BlockSpec((1, 1, d), lambda l, j: (l, 0, j)),
        ],
        out_specs=pl.BlockSpec((1, SUBLANES, d), lambda l, j: (l, 0, j)),
        out_shape=jax.ShapeDtypeStruct((DEPTH, SUBLANES, 3 * d), F32),
        compiler_params=pltpu.CompilerParams(dimension_semantics=("arbitrary", "arbitrary")),
    )(c_pad, ada_w, ada_b3)


def _rope_kernel(pos_ref, freq_ref, cos_ref, sin_ref):
    ang = pos_ref[0].astype(F32) * freq_ref[...]
    lane = lax.broadcasted_iota(jnp.int32, ang.shape, 1)
    is_t1 = (lane >= ROPE_LO) & (lane < ROPE_LO + ROPE_HALF)
    is_t2 = (lane >= ROPE_LO + ROPE_HALF) & (lane < ROPE_LO + QK_ROPE)
    c = jnp.cos(ang)
    s = jnp.sin(ang)
    cos_ref[0] = jnp.where(is_t1 | is_t2, c, 1.0)
    sin_ref[0] = jnp.where(is_t1, -s, jnp.where(is_t2, s, 0.0))


def _rope_call(pos3, freq_row):
    b, s, _ = pos3.shape
    ts = 512
    spec = pl.BlockSpec((1, ts, LANES), lambda i, j: (i, j, 0))
    return pl.pallas_call(
        _rope_kernel,
        grid=(b, s // ts),
        in_specs=[
            pl.BlockSpec((1, ts, 1), lambda i, j: (i, j, 0)),
            pl.BlockSpec((1, LANES), lambda i, j: (0, 0)),
        ],
        out_specs=[spec, spec],
        out_shape=[jax.ShapeDtypeStruct((b, s, LANES), F32)] * 2,
        compiler_params=pltpu.CompilerParams(dimension_semantics=("arbitrary", "arbitrary")),
    )(pos3, freq_row)


def _even_in_kernel(x_ref, shift_ref, scale_ref, png_ref, win_ref, scw_ref, scb_ref,
                    qg_ref, kvg_ref, wuq_ref, wuk_ref, wuvt_ref, cos_ref, sin_ref,
                    ya_ref, bg_ref, q_ref, k_ref, vt_ref, ubuf):
    tm = x_ref.shape[1]

    @pl.when(pl.program_id(1) == 0)
    def _():
        ubuf[0:SUBLANES, :] = jnp.zeros((SUBLANES, SC_WIDTH), F32)

    x = x_ref[0]
    h = _rms(x, png_ref[...]) * (1.0 + scale_ref[0]) + shift_ref[0]
    hb = h.astype(BF16)

    def proj(off, width):
        return jnp.dot(hb, win_ref[:, off:off + width], preferred_element_type=F32)

    cos_t = cos_ref[0]
    sin_t = sin_ref[0]
    lane = lax.broadcasted_iota(jnp.int32, (tm, LANES), 1)
    is_t2 = lane >= ROPE_LO + ROPE_HALF

    def rope(t):
        return t * cos_t + _swap_halves(t, is_t2) * sin_t

    z_kv = proj(OFF_CKV, KV_LORA + HEAD_PAD)
    ckv = _rms(z_kv[:, 0:KV_LORA], kvg_ref[...]).astype(BF16)
    vt = lax.dot_general(wuvt_ref[...], ckv, (((1,), (1,)), ((), ())),
                         preferred_element_type=F32).astype(BF16)
    for j in range(tm // TK):
        vt_ref[0, j] = vt[:, j * TK:(j + 1) * TK]
    kn = jnp.dot(ckv, wuk_ref[...], preferred_element_type=F32)
    kr = rope(z_kv[:, KV_LORA:KV_LORA + HEAD_PAD])
    for hd in range(MLA_HEADS):
        sl = slice(hd * HEAD_PAD, (hd + 1) * HEAD_PAD)
        k_ref[0, :, sl] = (kn[:, sl] + kr).astype(BF16)

    cq = _rms(proj(OFF_CQ, Q_LORA), qg_ref[...]).astype(BF16)
    qf = jnp.dot(cq, wuq_ref[...], preferred_element_type=F32)
    q_scale = math.log2(math.e) / math.sqrt(QK_NOPE + QK_ROPE)
    for hd in range(MLA_HEADS):
        sl = slice(hd * HEAD_PAD, (hd + 1) * HEAD_PAD)
        q_ref[0, :, sl] = (rope(qf[:, sl]) * q_scale).astype(BF16)

    u = proj(OFF_AC, SC_WIDTH) * proj(OFF_AX, SC_WIDTH)
    ubuf[SUBLANES:SUBLANES + tm, :] = u
    conv = (scw_ref[2:3, :] * u
            + scw_ref[1:2, :] * ubuf[SUBLANES - 1:SUBLANES - 1 + tm, :]
            + scw_ref[0:1, :] * ubuf[SUBLANES - 2:SUBLANES - 2 + tm, :]
            + scb_ref[...])
    ubuf[0:SUBLANES, :] = ubuf[tm:tm + SUBLANES, :]
    y_a = proj(OFF_AB, SC_WIDTH) * conv * _silu(proj(OFF_AG, SC_WIDTH))
    ya_ref[0] = y_a.astype(BF16)
    bg_ref[0] = _silu(proj(OFF_BG, MLA_WIDTH)).astype(BF16)


def _even_in_call(x, mod_l, png, win, scw, scb, qg, kvg, wuq, wuk, wuvt, cos_t, sin_t):
    b, s, d = x.shape
    hp = MLA_HEADS * HEAD_PAD
    tile = lambda w: pl.BlockSpec((1, TM, w), lambda i, j: (i, j, 0))
    whole = lambda a: pl.BlockSpec(a.shape, lambda i, j: (0,) * a.ndim)
    modspec = lambda k: pl.BlockSpec((1, 1, d), lambda i, j: (i, 0, k))
    return pl.pallas_call(
        _even_in_kernel,
        grid=(b, s // TM),
        in_specs=[tile(d), modspec(0), modspec(1), whole(png), whole(win), whole(scw), whole(scb),
                  whole(qg), whole(kvg), whole(wuq), whole(wuk), whole(wuvt), tile(LANES), tile(LANES)],
        out_specs=[tile(SC_WIDTH), tile(MLA_WIDTH), tile(hp), tile(hp),
                   pl.BlockSpec((1, TM // TK, MLA_WIDTH, TK), lambda i, j: (i, j, 0, 0))],
        out_shape=[jax.ShapeDtypeStruct((b, s, SC_WIDTH), BF16),
                   jax.ShapeDtypeStruct((b, s, MLA_WIDTH), BF16),
                   jax.ShapeDtypeStruct((b, s, hp), BF16),
                   jax.ShapeDtypeStruct((b, s, hp), BF16),
                   jax.ShapeDtypeStruct((b, s // TK, MLA_WIDTH, TK), BF16)],
        scratch_shapes=[pltpu.VMEM((TM + SUBLANES, SC_WIDTH), F32)],
        compiler_params=pltpu.CompilerParams(dimension_semantics=("arbitrary", "arbitrary"),
                                             vmem_limit_bytes=VMEM_LIMIT),
    )(x, mod_l, mod_l, png, win, scw, scb, qg, kvg, wuq, wuk, wuvt, cos_t, sin_t)


def _attn_out_kernel(q_ref, k_ref, vt_ref, bg_ref, x_ref, ya_ref, gate_ref, pog_ref, wa_ref, wb_ref,
                     o_ref, m_s, l_s, acc_s, yb_s, s_pre):
    qi = pl.program_id(1)
    m_s[...] = jnp.full(m_s.shape, NEG_BIG, F32)
    l_s[...] = jnp.zeros(l_s.shape, F32)
    acc_s[...] = jnp.zeros(acc_s.shape, F32)

    key = lax.broadcasted_iota(jnp.int32, (TK, TQ), 0)
    qry = lax.broadcasted_iota(jnp.int32, (TK, TQ), 1)
    diag_ok = (key // CHUNK) <= (qry // CHUNK)

    def scores(kb, hd):
        rows = pl.ds(pl.multiple_of(kb * TK, TK), TK)
        sl = slice(hd * HEAD_PAD, (hd + 1) * HEAD_PAD)
        return lax.dot_general(k_ref[0, rows, sl], q_ref[0, :, sl], (((1,), (1,)), ((), ())),
                               preferred_element_type=F32)

    def block(kb, masked):
        def update(hd, s):
            if masked:
                s = jnp.where(diag_ok, s, NEG_BIG)
            m_prev = m_s[hd]
            m_new = jnp.maximum(m_prev, jnp.max(s, axis=0, keepdims=True))
            alpha = jnp.exp2(m_prev - m_new)
            p = jnp.exp2(s - m_new)
            l_s[hd] = alpha * l_s[hd] + jnp.sum(p, axis=0, keepdims=True)
            pv = jnp.dot(vt_ref[0, kb, hd * V_HEAD:(hd + 1) * V_HEAD, :], p.astype(BF16),
                         preferred_element_type=F32)
            acc_s[hd] = alpha * acc_s[hd] + pv
            m_s[hd] = m_new

        pending = {hd: s_pre[hd] for hd in range(QK_AHEAD)}
        for hd in range(MLA_HEADS):
            ahead = hd + QK_AHEAD
            if ahead < MLA_HEADS:
                pending[ahead] = scores(kb, ahead)
            elif not masked:
                s_pre[ahead - MLA_HEADS] = scores(kb + 1, ahead - MLA_HEADS)
            update(hd, pending.pop(hd))

    def body(kb, carry):
        block(kb, False)
        return carry

    for hd in range(QK_AHEAD):
        s_pre[hd] = scores(0, hd)
    lax.fori_loop(0, qi, body, 0)
    block(qi, True)

    for pair in range(MLA_HEADS // 2):
        both = jnp.concatenate(
            [acc_s[2 * pair] * (1.0 / l_s[2 * pair]), acc_s[2 * pair + 1] * (1.0 / l_s[2 * pair + 1])],
            axis=0)
        cols = slice(pair * LANES, (pair + 1) * LANES)
        yb_s[:, cols] = (both.T * bg_ref[0, :, cols].astype(F32)).astype(BF16)
    y = (jnp.dot(ya_ref[0], wa_ref[...], preferred_element_type=F32)
         + jnp.dot(yb_s[...], wb_ref[...], preferred_element_type=F32))
    o_ref[0] = x_ref[0] + gate_ref[0] * _rms(y, pog_ref[...])


def _attn_out_call(q, k, vt, bg, x, ya, mod_l, pog, wa, wb):
    b, s, d = x.shape
    hp = MLA_HEADS * HEAD_PAD
    tile = lambda w: pl.BlockSpec((1, TQ, w), lambda i, j: (i, j, 0))
    whole = lambda a: pl.BlockSpec(a.shape, lambda i, j: (0,) * a.ndim)
    return pl.pallas_call(
        _attn_out_kernel,
        grid=(b, s // TQ),
        in_specs=[tile(hp),
                  pl.BlockSpec((1, s, hp), lambda i, j: (i, 0, 0)),
                  pl.BlockSpec((1, s // TK, MLA_WIDTH, TK), lambda i, j: (i, 0, 0, 0)),
                  tile(MLA_WIDTH), tile(d), tile(SC_WIDTH),
                  pl.BlockSpec((1, 1, d), lambda i, j: (i, 0, 2)), whole(pog), whole(wa), whole(wb)],
        out_specs=tile(d),
        out_shape=jax.ShapeDtypeStruct((b, s, d), F32),
        scratch_shapes=[pltpu.VMEM((MLA_HEADS, 1, TQ), F32),
                        pltpu.VMEM((MLA_HEADS, 1, TQ), F32),
                        pltpu.VMEM((MLA_HEADS, V_HEAD, TQ), F32),
                        pltpu.VMEM((TQ, MLA_WIDTH), BF16),
                        pltpu.VMEM((QK_AHEAD, TK, TQ), F32)],
        compiler_params=pltpu.CompilerParams(dimension_semantics=("arbitrary", "arbitrary"),
                                             vmem_limit_bytes=VMEM_LIMIT),
    )(q, k, vt, bg, x, ya, mod_l, pog, wa, wb)


def _odd_kernel(x_ref, shift_ref, scale_ref, gate_ref, png_ref, pog_ref, win_ref, cw_ref, cb_ref,
                lg_ref, lb_ref, wout_ref, o_ref, ubuf, vbuf, gbuf):
    tm = x_ref.shape[1]
    cw = CONF_WIDTH

    @pl.when(pl.program_id(1) == 0)
    def _():
        ubuf[0:CONF_HALO, :] = jnp.zeros((CONF_HALO, cw), F32)

    x = x_ref[0]
    hb = (_rms(x, png_ref[...]) * (1.0 + scale_ref[0]) + shift_ref[0]).astype(BF16)

    def proj(off):
        return jnp.dot(hb, win_ref[:, off:off + CONF_GROUP], preferred_element_type=F32)

    for c0 in range(0, cw, CONF_GROUP):
        ubuf[CONF_HALO:CONF_HALO + tm, c0:c0 + CONF_GROUP] = proj(c0) * jax.nn.sigmoid(proj(cw + c0))
        gbuf[:, c0:c0 + CONF_GROUP] = _silu(proj(2 * cw + c0))
        for cc in range(c0, c0 + CONF_GROUP, LANES):
            cs = slice(cc, cc + LANES)
            for rr in range(0, tm, CONV_ROWS):
                acc = jnp.broadcast_to(cb_ref[:, cs], (CONV_ROWS, LANES))
                window = ubuf[rr:rr + CONV_ROWS + CONF_HALO, cs]
                for r in range(SUBLANES):
                    shifted = window if r == 0 else pltpu.roll(window, r, 0)
                    for a in range(CONF_HALO // SUBLANES):
                        j = SUBLANES * a + r
                        if j >= CONF_KERNEL:
                            continue
                        kk = CONF_KERNEL - 1 - j
                        start = CONF_HALO - SUBLANES * a
                        acc = acc + cw_ref[kk:kk + 1, cs] * shifted[start:start + CONV_ROWS]
                vbuf[rr:rr + CONV_ROWS, cs] = acc
    ubuf[0:CONF_HALO, :] = ubuf[tm:tm + CONF_HALO, :]

    v = vbuf[...]
    mu = jnp.mean(v, axis=-1, keepdims=True)
    var = jnp.mean(jnp.square(v - mu), axis=-1, keepdims=True)
    rstd = lax.rsqrt(var + EPS)
    y = jnp.zeros((tm, cw), F32)
    for c0 in range(0, cw, CONF_GROUP):
        cs = slice(c0, c0 + CONF_GROUP)
        ln = (vbuf[:, cs] - mu) * rstd * lg_ref[:, cs] + lb_ref[:, cs]
        t = _silu(ln) * gbuf[:, cs]
        y = y + jnp.dot(t.astype(BF16), wout_ref[cs, :], preferred_element_type=F32)
    o_ref[0] = x + gate_ref[0] * _rms(y, pog_ref[...])


def _odd_call(x, mod_l, png, pog, win, cw, cb, lg, lb, wout):
    b, s, d = x.shape
    tile = pl.BlockSpec((1, TM, d), lambda i, j: (i, j, 0))
    whole = lambda a: pl.BlockSpec(a.shape, lambda i, j: (0,) * a.ndim)
    modspec = lambda k: pl.BlockSpec((1, 1, d), lambda i, j: (i, 0, k))
    return pl.pallas_call(
        _odd_kernel,
        grid=(b, s // TM),
        in_specs=[tile, modspec(0), modspec(1), modspec(2), whole(png), whole(pog), whole(win),
                  whole(cw), whole(cb), whole(lg), whole(lb), whole(wout)],
        out_specs=tile,
        out_shape=jax.ShapeDtypeStruct((b, s, d), F32),
        scratch_shapes=[pltpu.VMEM((TM + CONF_HALO, CONF_WIDTH), F32),
                        pltpu.VMEM((TM, CONF_WIDTH), F32),
                        pltpu.VMEM((TM, CONF_WIDTH), F32)],
        compiler_params=pltpu.CompilerParams(dimension_semantics=("arbitrary", "arbitrary"),
                                             vmem_limit_bytes=VMEM_LIMIT),
    )(x, mod_l, mod_l, mod_l, png, pog, win, cw, cb, lg, lb, wout)


def _pack_even_w_in(w):
    d = w.shape[0]
    a = w[:, :4 * SC_WIDTH]
    o = 4 * SC_WIDTH
    c_q = w[:, o:o + Q_LORA]
    c_kv = w[:, o + Q_LORA:o + Q_LORA + KV_LORA]
    k_r = w[:, o + Q_LORA + KV_LORA:o + Q_LORA + KV_LORA + QK_ROPE]
    b_g = w[:, o + Q_LORA + KV_LORA + QK_ROPE:]
    k_r = jnp.concatenate([jnp.zeros((d, ROPE_LO), w.dtype), k_r,
                           jnp.zeros((d, HEAD_PAD - ROPE_LO - QK_ROPE), w.dtype)], axis=1)
    return jnp.concatenate([a, b_g, c_q, c_kv, k_r], axis=1).astype(BF16)


def _pad_heads(w, width):
    k = w.shape[0]
    w = w.reshape(k, MLA_HEADS, width)
    w = jnp.pad(w, ((0, 0), (0, 0), (0, HEAD_PAD - width)))
    return w.reshape(k, MLA_HEADS * HEAD_PAD).astype(BF16)


def kernel(x, c, positions, ada_w, ada_b, pre_norm_g, post_norm_g, even_w_in, even_sc_conv_w, even_sc_conv_b, even_q_norm_g, even_kv_norm_g, even_w_uq, even_w_ukv, even_w_out, odd_w_in, odd_conv_w, odd_conv_b, odd_ln_g, odd_ln_b, odd_w_out):
    b, s, d = x.shape
    assert d == D_MODEL and s % TM == 0 and s % TQ == 0 and TM % TK == 0 and b <= SUBLANES
    assert TQ % CHUNK == 0 and TK % CHUNK == 0 and TQ == TK and TM % CONV_ROWS == 0

    c_pad = jnp.pad(c, ((0, SUBLANES - b), (0, 0)))
    mod = _mod_call(c_pad, ada_w, ada_b.reshape(DEPTH, 1, 3 * d))

    inv_freq = 1.0 / (ROPE_THETA ** (jnp.arange(0, QK_ROPE, 2, dtype=F32) / QK_ROPE))
    freq_row = jnp.concatenate([jnp.zeros((ROPE_LO,), F32), inv_freq, inv_freq,
                                jnp.zeros((HEAD_PAD - ROPE_LO - QK_ROPE,), F32)]).reshape(1, LANES)
    cos_t, sin_t = _rope_call(positions.reshape(b, s, 1), freq_row)

    row = lambda a: a.reshape(1, -1)
    for layer in range(DEPTH):
        mod_l = mod[layer, :b].reshape(b, 1, 3 * d)
        png, pog = row(pre_norm_g[layer]), row(post_norm_g[layer])
        i = layer // 2
        if layer % 2 == 0:
            w_ukv = even_w_ukv[i].reshape(KV_LORA, MLA_HEADS, QK_NOPE + V_HEAD)
            wuk = _pad_heads(w_ukv[:, :, :QK_NOPE].reshape(KV_LORA, -1), QK_NOPE)
            wuvt = w_ukv[:, :, QK_NOPE:].reshape(KV_LORA, MLA_WIDTH).T.astype(BF16)
            wuq = _pad_heads(even_w_uq[i], QK_NOPE + QK_ROPE)
            ya, bg, q, k, vt = _even_in_call(
                x, mod_l, png, _pack_even_w_in(even_w_in[i]), even_sc_conv_w[i],
                row(even_sc_conv_b[i]), row(even_q_norm_g[i]), row(even_kv_norm_g[i]),
                wuq, wuk, wuvt, cos_t, sin_t)
            w_out = even_w_out[i].astype(BF16)
            x = _attn_out_call(q, k, vt, bg, x, ya, mod_l, pog, w_out[:SC_WIDTH], w_out[SC_WIDTH:])
        else:
            x = _odd_call(x, mod_l, png, pog, odd_w_in[i].astype(BF16), odd_conv_w[i],
                          row(odd_conv_b[i]), row(odd_ln_g[i]), row(odd_ln_b[i]),
                          odd_w_out[i].astype(BF16))
    return x
```

```python
import math

import jax
import jax.numpy as jnp
from jax import lax
from jax.experimental import pallas as pl
from jax.experimental.pallas import tpu as pltpu

D_MODEL = 1024
DEPTH = 4
CHUNK = 64
SC_WIDTH = 512
SC_KERNEL = 3
MLA_HEADS = 8
QK_NOPE = 64
QK_ROPE = 32
V_HEAD = 64
Q_LORA = 256
KV_LORA = 128
MLA_WIDTH = MLA_HEADS * V_HEAD
VT_ROWS = V_HEAD + 16
ROPE_THETA = 10000.0
CONF_WIDTH = D_MODEL
CONF_KERNEL = 31
EPS = 1e-6

LANES = 128
HEAD_PAD = LANES
ROPE_LO = QK_NOPE
ROPE_HALF = QK_ROPE // 2
SUBLANES = 8
CONF_HALO = 32

OFF_AB, OFF_AC, OFF_AX, OFF_AG, OFF_BG = 0, 512, 1024, 1536, 2048
OFF_CQ = 2560
OFF_CKV = OFF_CQ + Q_LORA
OFF_KR = OFF_CKV + KV_LORA
EVEN_IN_PAD = OFF_KR + HEAD_PAD

TM = 512
CONF_GROUP = 256
TQ = 256
TK = 256
CONV_SEG = 12
QK_PARKED = 4
QK_AHEAD = 4
NEG_BIG = -1e30
VMEM_LIMIT = 56 * 1024 * 1024

F32 = jnp.float32
BF16 = jnp.bfloat16


def _rms(x, g):
    return x * lax.rsqrt(jnp.mean(x * x, axis=-1, keepdims=True) + EPS) * g


def _silu(x):
    return x * jax.nn.sigmoid(x)


def _swap_halves(t, is_t2):
    return jnp.where(is_t2, pltpu.roll(t, ROPE_HALF, 1), pltpu.roll(t, LANES - ROPE_HALF, 1))


def _mod_kernel(c_ref, w_ref, b_ref, o_ref):
    a = _silu(c_ref[...])
    o_ref[0] = jnp.dot(a.astype(BF16), w_ref[0].astype(BF16), preferred_element_type=F32) + b_ref[0]


def _mod_call(c_pad, ada_w, ada_b3):
    d = D_MODEL
    return pl.pallas_call(
        _mod_kernel,
        grid=(DEPTH, 3),
        in_specs=[
            pl.BlockSpec((SUBLANES, d), lambda l, j: (0, 0)),
            pl.BlockSpec((1, d, d), lambda l, j: (l, 0, j)),
            pl.BlockSpec((1, 1, d), lambda l, j: (l, 0, j)),
        ],
        out_specs=pl.BlockSpec((1, SUBLANES, d), lambda l, j: (l, 0, j)),
        out_shape=jax.ShapeDtypeStruct((DEPTH, SUBLANES, 3 * d), F32),
        compiler_params=pltpu.CompilerParams(dimension_semantics=("arbitrary", "arbitrary")),
    )(c_pad, ada_w, ada_b3)


def _rope_kernel(pos_ref, freq_ref, cos_ref, sin_ref):
    ang = pos_ref[0].astype(F32) * freq_ref[...]
    lane = lax.broadcasted_iota(jnp.int32, ang.shape, 1)
    is_t1 = (lane >= ROPE_LO) & (lane < ROPE_LO + ROPE_HALF)
    is_t2 = (lane >= ROPE_LO + ROPE_HALF) & (lane < ROPE_LO + QK_ROPE)
    c = jnp.cos(ang)
    s = jnp.sin(ang)
    cos_ref[0] = jnp.where(is_t1 | is_t2, c, 1.0)
    sin_ref[0] = jnp.where(is_t1, -s, jnp.where(is_t2, s, 0.0))


def _rope_call(pos3, freq_row):
    b, s, _ = pos3.shape
    ts = 512
    spec = pl.BlockSpec((1, ts, LANES), lambda i, j: (i, j, 0))
    return pl.pallas_call(
        _rope_kernel,
        grid=(b, s // ts),
        in_specs=[
            pl.BlockSpec((1, ts, 1), lambda i, j: (i, j, 0)),
            pl.BlockSpec((1, LANES), lambda i, j: (0, 0)),
        ],
        out_specs=[spec, spec],
        out_shape=[jax.ShapeDtypeStruct((b, s, LANES), F32)] * 2,
        compiler_params=pltpu.CompilerParams(dimension_semantics=("arbitrary", "arbitrary")),
    )(pos3, freq_row)


def _even_in_kernel(x_ref, shift_ref, scale_ref, png_ref, win_ref, scw_ref, scb_ref,
                    qg_ref, kvg_ref, wuq_ref, wuk_ref, wuvt_ref, cos_ref, sin_ref,
                    ya_ref, bg_ref, q_ref, k_ref, vt_ref, ubuf):
    tm = x_ref.shape[1]

    @pl.when(pl.program_id(1) == 0)
    def _():
        ubuf[0:SUBLANES, :] = jnp.zeros((SUBLANES, SC_WIDTH), F32)

    x = x_ref[0]
    h = _rms(x, png_ref[...]) * (1.0 + scale_ref[0]) + shift_ref[0]
    hb = h.astype(BF16)

    def proj(off, width):
        return jnp.dot(hb, win_ref[:, off:off + width], preferred_element_type=F32)

    cos_t = cos_ref[0]
    sin_t = sin_ref[0]
    lane = lax.broadcasted_iota(jnp.int32, (tm, LANES), 1)
    is_t2 = lane >= ROPE_LO + ROPE_HALF

    def rope(t):
        return t * cos_t + _swap_halves(t, is_t2) * sin_t

    z_kv = proj(OFF_CKV, KV_LORA + HEAD_PAD)
    ckv = _rms(z_kv[:, 0:KV_LORA], kvg_ref[...]).astype(BF16)
    vt = lax.dot_general(wuvt_ref[...], ckv, (((1,), (1,)), ((), ())),
                         preferred_element_type=F32).astype(BF16)
    ones_rows = (lax.broadcasted_iota(jnp.int32, (VT_ROWS - V_HEAD, tm), 0) == 0).astype(BF16)
    vt_aug = jnp.concatenate(
        [part for hd in range(MLA_HEADS) for part in (vt[hd * V_HEAD:(hd + 1) * V_HEAD], ones_rows)],
        axis=0)
    for j in range(tm // TK):
        vt_ref[0, j] = vt_aug[:, j * TK:(j + 1) * TK]
    kn = jnp.dot(ckv, wuk_ref[...], preferred_element_type=F32)
    kr = rope(z_kv[:, KV_LORA:KV_LORA + HEAD_PAD])
    for hd in range(MLA_HEADS):
        sl = slice(hd * HEAD_PAD, (hd + 1) * HEAD_PAD)
        k_ref[0, :, sl] = (kn[:, sl] + kr).astype(BF16)

    cq = _rms(proj(OFF_CQ, Q_LORA), qg_ref[...]).astype(BF16)
    qf = jnp.dot(cq, wuq_ref[...], preferred_element_type=F32)
    q_scale = math.log2(math.e) / math.sqrt(QK_NOPE + QK_ROPE)
    for hd in range(MLA_HEADS):
        sl = slice(hd * HEAD_PAD, (hd + 1) * HEAD_PAD)
        q_ref[0, :, sl] = (rope(qf[:, sl]) * q_scale).astype(BF16)

    u = proj(OFF_AC, SC_WIDTH) * proj(OFF_AX, SC_WIDTH)
    ubuf[SUBLANES:SUBLANES + tm, :] = u
    conv = (scw_ref[2:3, :] * u
            + scw_ref[1:2, :] * ubuf[SUBLANES - 1:SUBLANES - 1 + tm, :]
            + scw_ref[0:1, :] * ubuf[SUBLANES - 2:SUBLANES - 2 + tm, :]
            + scb_ref[...])
    ubuf[0:SUBLANES, :] = ubuf[tm:tm + SUBLANES, :]
    y_a = proj(OFF_AB, SC_WIDTH) * conv * _silu(proj(OFF_AG, SC_WIDTH))
    ya_ref[0] = y_a.astype(BF16)
    bg_ref[0] = _silu(proj(OFF_BG, MLA_WIDTH)).astype(BF16)


def _even_in_call(x, mod_l, png, win, scw, scb, qg, kvg, wuq, wuk, wuvt, cos_t, sin_t):
    b, s, d = x.shape
    hp = MLA_HEADS * HEAD_PAD
    tile = lambda w: pl.BlockSpec((1, TM, w), lambda i, j: (i, j, 0))
    whole = lambda a: pl.BlockSpec(a.shape, lambda i, j: (0,) * a.ndim)
    modspec = lambda k: pl.BlockSpec((1, 1, d), lambda i, j: (i, 0, k))
    return pl.pallas_call(
        _even_in_kernel,
        grid=(b, s // TM),
        in_specs=[tile(d), modspec(0), modspec(1), whole(png), whole(win), whole(scw), whole(scb),
                  whole(qg), whole(kvg), whole(wuq), whole(wuk), whole(wuvt), tile(LANES), tile(LANES)],
        out_specs=[tile(SC_WIDTH), tile(MLA_WIDTH), tile(hp), tile(hp),
                   pl.BlockSpec((1, TM // TK, MLA_HEADS * VT_ROWS, TK), lambda i, j: (i, j, 0, 0))],
        out_shape=[jax.ShapeDtypeStruct((b, s, SC_WIDTH), BF16),
                   jax.ShapeDtypeStruct((b, s, MLA_WIDTH), BF16),
                   jax.ShapeDtypeStruct((b, s, hp), BF16),
                   jax.ShapeDtypeStruct((b, s, hp), BF16),
                   jax.ShapeDtypeStruct((b, s // TK, MLA_HEADS * VT_ROWS, TK), BF16)],
        scratch_shapes=[pltpu.VMEM((TM + SUBLANES, SC_WIDTH), F32)],
        compiler_params=pltpu.CompilerParams(dimension_semantics=("arbitrary", "arbitrary"),
                                             vmem_limit_bytes=VMEM_LIMIT),
    )(x, mod_l, mod_l, png, win, scw, scb, qg, kvg, wuq, wuk, wuvt, cos_t, sin_t)


def _attn_out_kernel(q_ref, k_ref, vt_ref, bg_ref, x_ref, ya_ref, gate_ref, pog_ref, wa_ref, wb_ref,
                     o_ref, m_s, acc_s, yb_s, s_pre):
    qi = pl.program_id(1)
    m_s[...] = jnp.full(m_s.shape, NEG_BIG, F32)
    acc_s[...] = jnp.zeros(acc_s.shape, F32)

    key_chunk = lax.broadcasted_iota(jnp.int32, (TK, TQ), 0) // CHUNK
    qry_chunk = lax.broadcasted_iota(jnp.int32, (TK, TQ), 1) // CHUNK

    def scores(kb, hd):
        rows = pl.ds(pl.multiple_of(kb * TK, TK), TK)
        sl = slice(hd * HEAD_PAD, (hd + 1) * HEAD_PAD)
        return lax.dot_general(k_ref[0, rows, sl], q_ref[0, :, sl], (((1,), (1,)), ((), ())),
                               preferred_element_type=F32)

    def block(kb, diag=None, last=False):
        masked = diag is not None

        def update(hd, s):
            if masked:
                s = jnp.where(key_chunk + diag * (TK // CHUNK) <= qry_chunk, s, NEG_BIG)
            m_prev = m_s[hd]
            m_new = jnp.maximum(m_prev, jnp.max(s, axis=0, keepdims=True))
            alpha = jnp.exp2(m_prev - m_new)
            p = jnp.exp2(s - m_new)
            pv = jnp.dot(vt_ref[0, kb, hd * VT_ROWS:(hd + 1) * VT_ROWS, :], p.astype(BF16),
                         preferred_element_type=F32)
            acc_s[hd] = alpha * acc_s[hd] + pv
            m_s[hd] = m_new

        pending = {hd: s_pre[hd] for hd in range(QK_PARKED)}
        for hd in range(QK_PARKED, QK_AHEAD):
            pending[hd] = scores(kb, hd)
        for hd in range(MLA_HEADS):
            ahead = hd + QK_AHEAD
            if ahead < MLA_HEADS:
                pending[ahead] = scores(kb, ahead)
            elif not last and ahead - MLA_HEADS < QK_PARKED:
                s_pre[ahead - MLA_HEADS] = scores(kb + 1, ahead - MLA_HEADS)
            update(hd, pending.pop(hd))

    def pair_body(kk, carry):
        block(2 * kk)
        block(2 * kk + 1)
        return carry

    def single_body(kb, carry):
        block(kb)
        return carry

    span = TQ // TK
    visible = qi * span
    for hd in range(QK_PARKED):
        s_pre[hd] = scores(0, hd)
    lax.fori_loop(0, lax.shift_right_logical(visible, 1), pair_body, 0)
    lax.fori_loop(visible - (visible & 1), visible, single_body, 0)
    for d in range(span):
        block(qi * span + d, diag=d, last=(d == span - 1))

    for pair in range(MLA_HEADS // 2):
        both = jnp.concatenate(
            [acc_s[hd, 0:V_HEAD, :] * (1.0 / acc_s[hd, V_HEAD:V_HEAD + 1, :])
             for hd in (2 * pair, 2 * pair + 1)], axis=0)
        cols = slice(pair * LANES, (pair + 1) * LANES)
        yb_s[:, cols] = (both.T * bg_ref[0, :, cols].astype(F32)).astype(BF16)
    y = (jnp.dot(ya_ref[0], wa_ref[...], preferred_element_type=F32)
         + jnp.dot(yb_s[...], wb_ref[...], preferred_element_type=F32))
    o_ref[0] = x_ref[0] + gate_ref[0] * _rms(y, pog_ref[...])


def _attn_out_call(q, k, vt, bg, x, ya, mod_l, pog, wa, wb):
    b, s, d = x.shape
    hp = MLA_HEADS * HEAD_PAD
    tile = lambda w: pl.BlockSpec((1, TQ, w), lambda i, j: (i, j, 0))
    whole = lambda a: pl.BlockSpec(a.shape, lambda i, j: (0,) * a.ndim)
    return pl.pallas_call(
        _attn_out_kernel,
        grid=(b, s // TQ),
        in_specs=[tile(hp),
                  pl.BlockSpec((1, s, hp), lambda i, j: (i, 0, 0)),
                  pl.BlockSpec((1, s // TK, MLA_HEADS * VT_ROWS, TK), lambda i, j: (i, 0, 0, 0)),
                  tile(MLA_WIDTH), tile(d), tile(SC_WIDTH),
                  pl.BlockSpec((1, 1, d), lambda i, j: (i, 0, 2)), whole(pog), whole(wa), whole(wb)],
        out_specs=tile(d),
        out_shape=jax.ShapeDtypeStruct((b, s, d), F32),
        scratch_shapes=[pltpu.VMEM((MLA_HEADS, 1, TQ), F32),
                        pltpu.VMEM((MLA_HEADS, VT_ROWS, TQ), F32),
                        pltpu.VMEM((TQ, MLA_WIDTH), BF16),
                        pltpu.VMEM((QK_PARKED, TK, TQ), F32)],
        compiler_params=pltpu.CompilerParams(dimension_semantics=("arbitrary", "arbitrary"),
                                             vmem_limit_bytes=VMEM_LIMIT),
    )(q, k, vt, bg, x, ya, mod_l, pog, wa, wb)


def _odd_kernel(x_ref, shift_ref, scale_ref, gate_ref, png_ref, pog_ref, win_ref, cw_ref, cb_ref,
                lg_ref, lb_ref, wout_ref, o_ref, ubuf, vbuf, gbuf):
    tm = x_ref.shape[1]
    cw = CONF_WIDTH

    @pl.when(pl.program_id(1) == 0)
    def _():
        ubuf[:, 0:CONF_HALO, :] = jnp.zeros((cw // LANES, CONF_HALO, LANES), F32)

    x = x_ref[0]
    hb = (_rms(x, png_ref[...]) * (1.0 + scale_ref[0]) + shift_ref[0]).astype(BF16)

    def proj(off):
        return jnp.dot(hb, win_ref[:, off:off + CONF_GROUP], preferred_element_type=F32)

    groups = []
    g0 = 0
    while g0 < tm:
        seg = CONV_SEG if tm - g0 >= SUBLANES * CONV_SEG else (tm - g0) // SUBLANES
        groups.append((g0, seg))
        g0 += SUBLANES * seg

    for c0 in range(0, cw, CONF_GROUP):
        u = proj(c0) * jax.nn.sigmoid(proj(cw + c0))
        gbuf[:, c0:c0 + CONF_GROUP] = _silu(proj(2 * cw + c0))
        for n in range(CONF_GROUP // LANES):
            slab = c0 // LANES + n
            cs = slice(slab * LANES, (slab + 1) * LANES)
            ubuf[slab, CONF_HALO:CONF_HALO + tm, :] = u[:, n * LANES:(n + 1) * LANES]
            for g0, seg in groups:
                taps = {m: ubuf[slab, pl.ds(CONF_HALO + g0 + m, SUBLANES, stride=seg), :]
                        for m in range(1 - CONF_KERNEL, seg)}
                acc = [jnp.broadcast_to(cb_ref[:, cs], (SUBLANES, LANES)) for _ in range(seg)]
                for j in range(CONF_KERNEL):
                    w_j = jnp.broadcast_to(cw_ref[CONF_KERNEL - 1 - j:CONF_KERNEL - j, cs], (SUBLANES, LANES))
                    for i in range(seg):
                        acc[i] = acc[i] + w_j * taps[i - j]
                for i in range(seg):
                    vbuf[slab, pl.ds(g0 + i, SUBLANES, stride=seg), :] = acc[i]
    ubuf[:, 0:CONF_HALO, :] = ubuf[:, tm:tm + CONF_HALO, :]

    n_slab = cw // LANES
    mu = sum(jnp.sum(vbuf[c], axis=-1, keepdims=True) for c in range(n_slab)) * (1.0 / cw)
    var = sum(jnp.sum(jnp.square(vbuf[c] - mu), axis=-1, keepdims=True) for c in range(n_slab)) * (1.0 / cw)
    rstd = lax.rsqrt(var + EPS)
    y = jnp.zeros((tm, cw), F32)
    for c0 in range(0, cw, CONF_GROUP):
        cs = slice(c0, c0 + CONF_GROUP)
        v = jnp.concatenate([vbuf[c0 // LANES + n] for n in range(CONF_GROUP // LANES)], axis=-1)
        ln = (v - mu) * rstd * lg_ref[:, cs] + lb_ref[:, cs]
        t = _silu(ln) * gbuf[:, cs]
        y = y + jnp.dot(t.astype(BF16), wout_ref[cs, :], preferred_element_type=F32)
    o_ref[0] = x + gate_ref[0] * _rms(y, pog_ref[...])


def _odd_call(x, mod_l, png, pog, win, cw, cb, lg, lb, wout):
    b, s, d = x.shape
    tile = pl.BlockSpec((1, TM, d), lambda i, j: (i, j, 0))
    whole = lambda a: pl.BlockSpec(a.shape, lambda i, j: (0,) * a.ndim)
    modspec = lambda k: pl.BlockSpec((1, 1, d), lambda i, j: (i, 0, k))
    return pl.pallas_call(
        _odd_kernel,
        grid=(b, s // TM),
        in_specs=[tile, modspec(0), modspec(1), modspec(2), whole(png), whole(pog), whole(win),
                  whole(cw), whole(cb), whole(lg), whole(lb), whole(wout)],
        out_specs=tile,
        out_shape=jax.ShapeDtypeStruct((b, s, d), F32),
        scratch_shapes=[pltpu.VMEM((CONF_WIDTH // LANES, TM + CONF_HALO, LANES), F32),
                        pltpu.VMEM((CONF_WIDTH // LANES, TM, LANES), F32),
                        pltpu.VMEM((TM, CONF_WIDTH), F32)],
        compiler_params=pltpu.CompilerParams(dimension_semantics=("arbitrary", "arbitrary"),
                                             vmem_limit_bytes=VMEM_LIMIT),
    )(x, mod_l, mod_l, mod_l, png, pog, win, cw, cb, lg, lb, wout)


def _pack_even_w_in(w):
    d = w.shape[0]
    a = w[:, :4 * SC_WIDTH]
    o = 4 * SC_WIDTH
    c_q = w[:, o:o + Q_LORA]
    c_kv = w[:, o + Q_LORA:o + Q_LORA + KV_LORA]
    k_r = w[:, o + Q_LORA + KV_LORA:o + Q_LORA + KV_LORA + QK_ROPE]
    b_g = w[:, o + Q_LORA + KV_LORA + QK_ROPE:]
    k_r = jnp.concatenate([jnp.zeros((d, ROPE_LO), w.dtype), k_r,
                           jnp.zeros((d, HEAD_PAD - ROPE_LO - QK_ROPE), w.dtype)], axis=1)
    return jnp.concatenate([a, b_g, c_q, c_kv, k_r], axis=1).astype(BF16)


def _pad_heads(w, width):
    k = w.shape[0]
    w = w.reshape(k, MLA_HEADS, width)
    w = jnp.pad(w, ((0, 0), (0, 0), (0, HEAD_PAD - width)))
    return w.reshape(k, MLA_HEADS * HEAD_PAD).astype(BF16)


def kernel(x, c, positions, ada_w, ada_b, pre_norm_g, post_norm_g, even_w_in, even_sc_conv_w, even_sc_conv_b, even_q_norm_g, even_kv_norm_g, even_w_uq, even_w_ukv, even_w_out, odd_w_in, odd_conv_w, odd_conv_b, odd_ln_g, odd_ln_b, odd_w_out):
    b, s, d = x.shape
    assert d == D_MODEL and s % TM == 0 and s % TQ == 0 and TM % TK == 0 and b <= SUBLANES
    assert TQ % CHUNK == 0 and TK % CHUNK == 0 and TQ % TK == 0
    assert CONV_SEG % SUBLANES and (TM % (SUBLANES * CONV_SEG)) % (SUBLANES * SUBLANES)

    c_pad = jnp.pad(c, ((0, SUBLANES - b), (0, 0)))
    mod = _mod_call(c_pad, ada_w, ada_b.reshape(DEPTH, 1, 3 * d))

    inv_freq = 1.0 / (ROPE_THETA ** (jnp.arange(0, QK_ROPE, 2, dtype=F32) / QK_ROPE))
    freq_row = jnp.concatenate([jnp.zeros((ROPE_LO,), F32), inv_freq, inv_freq,
                                jnp.zeros((HEAD_PAD - ROPE_LO - QK_ROPE,), F32)]).reshape(1, LANES)
    cos_t, sin_t = _rope_call(positions.reshape(b, s, 1), freq_row)

    row = lambda a: a.reshape(1, -1)
    for layer in range(DEPTH):
        mod_l = mod[layer, :b].reshape(b, 1, 3 * d)
        png, pog = row(pre_norm_g[layer]), row(post_norm_g[layer])
        i = layer // 2
        if layer % 2 == 0:
            w_ukv = even_w_ukv[i].reshape(KV_LORA, MLA_HEADS, QK_NOPE + V_HEAD)
            wuk = _pad_heads(w_ukv[:, :, :QK_NOPE].reshape(KV_LORA, -1), QK_NOPE)
            wuvt = w_ukv[:, :, QK_NOPE:].reshape(KV_LORA, MLA_WIDTH).T.astype(BF16)
            wuq = _pad_heads(even_w_uq[i], QK_NOPE + QK_ROPE)
            ya, bg, q, k, vt = _even_in_call(
                x, mod_l, png, _pack_even_w_in(even_w_in[i]), even_sc_conv_w[i],
                row(even_sc_conv_b[i]), row(even_q_norm_g[i]), row(even_kv_norm_g[i]),
                wuq, wuk, wuvt, cos_t, sin_t)
            w_out = even_w_out[i].astype(BF16)
            x = _attn_out_call(q, k, vt, bg, x, ya, mod_l, pog, w_out[:SC_WIDTH], w_out[SC_WIDTH:])
        else:
            x = _odd_call(x, mod_l, png, pog, odd_w_in[i].astype(BF16), odd_conv_w[i],
                          row(odd_conv_b[i]), row(odd_ln_g[i]), row(odd_ln_b[i]),
                          odd_w_out[i].astype(BF16))
    return x
```

```python
import math

import jax
import jax.numpy as jnp
from jax import lax
from jax.experimental import pallas as pl
from jax.experimental.pallas import tpu as pltpu

D_MODEL = 1024
DEPTH = 4
CHUNK = 64
SC_WIDTH = 512
SC_KERNEL = 3
MLA_HEADS = 8
QK_NOPE = 64
QK_ROPE = 32
V_HEAD = 64
Q_LORA = 256
KV_LORA = 128
MLA_WIDTH = MLA_HEADS * V_HEAD
VT_ROWS = V_HEAD + 16
ROPE_THETA = 10000.0
CONF_WIDTH = D_MODEL
CONF_KERNEL = 31
EPS = 1e-6

LANES = 128
HEAD_PAD = LANES
ROPE_LO = QK_NOPE
ROPE_HALF = QK_ROPE // 2
SUBLANES = 8
CONF_HALO = 32

OFF_AB, OFF_AC, OFF_AX, OFF_AG, OFF_BG = 0, 512, 1024, 1536, 2048
OFF_CQ = 2560
OFF_CKV = OFF_CQ + Q_LORA
OFF_KR = OFF_CKV + KV_LORA
EVEN_IN_PAD = OFF_KR + HEAD_PAD

TM = 512
CONF_GROUP = 256
TQ = 256
TK = 256
CONV_SEG_TAIL = 4
CONV_SEG = 12
QK_PARKED = 4
QK_AHEAD = 4
NEG_BIG = -1e30
VMEM_LIMIT = 56 * 1024 * 1024

F32 = jnp.float32
BF16 = jnp.bfloat16


def _rms(x, g):
    return x * lax.rsqrt(jnp.mean(x * x, axis=-1, keepdims=True) + EPS) * g


def _silu(x):
    return x * jax.nn.sigmoid(x)


def _swap_halves(t, is_t2):
    return jnp.where(is_t2, pltpu.roll(t, ROPE_HALF, 1), pltpu.roll(t, LANES - ROPE_HALF, 1))


def _mod_kernel(c_ref, w_ref, b_ref, o_ref):
    a = _silu(c_ref[...])
    o_ref[0] = jnp.dot(a.astype(BF16), w_ref[0].astype(BF16), preferred_element_type=F32) + b_ref[0]


def _mod_call(c_pad, ada_w, ada_b3):
    d = D_MODEL
    return pl.pallas_call(
        _mod_kernel,
        grid=(DEPTH,),
        in_specs=[
            pl.BlockSpec((SUBLANES, d), lambda l: (0, 0)),
            pl.BlockSpec((1, d, 3 * d), lambda l: (l, 0, 0)),
            pl.BlockSpec((1, 1, 3 * d), lambda l: (l, 0, 0)),
        ],
        out_specs=pl.BlockSpec((1, SUBLANES, 3 * d), lambda l: (l, 0, 0)),
        out_shape=jax.ShapeDtypeStruct((DEPTH, SUBLANES, 3 * d), F32),
        compiler_params=pltpu.CompilerParams(dimension_semantics=("arbitrary",),
                                             vmem_limit_bytes=VMEM_LIMIT),
    )(c_pad, ada_w, ada_b3)


def _rope_kernel(pos_ref, freq_ref, cos_ref, sin_ref):
    ang = pos_ref[0].astype(F32) * freq_ref[...]
    lane = lax.broadcasted_iota(jnp.int32, ang.shape, 1)
    is_t1 = (lane >= ROPE_LO) & (lane < ROPE_LO + ROPE_HALF)
    is_t2 = (lane >= ROPE_LO + ROPE_HALF) & (lane < ROPE_LO + QK_ROPE)
    c = jnp.cos(ang)
    s = jnp.sin(ang)
    cos_ref[0] = jnp.where(is_t1 | is_t2, c, 1.0)
    sin_ref[0] = jnp.where(is_t1, -s, jnp.where(is_t2, s, 0.0))


def _rope_call(pos3, freq_row):
    b, s, _ = pos3.shape
    ts = 512
    spec = pl.BlockSpec((1, ts, LANES), lambda i, j: (i, j, 0))
    return pl.pallas_call(
        _rope_kernel,
        grid=(b, s // ts),
        in_specs=[
            pl.BlockSpec((1, ts, 1), lambda i, j: (i, j, 0)),
            pl.BlockSpec((1, LANES), lambda i, j: (0, 0)),
        ],
        out_specs=[spec, spec],
        out_shape=[jax.ShapeDtypeStruct((b, s, LANES), F32)] * 2,
        compiler_params=pltpu.CompilerParams(dimension_semantics=("arbitrary", "arbitrary")),
    )(pos3, freq_row)


def _even_in_kernel(x_ref, shift_ref, scale_ref, png_ref, win_ref, scw_ref, scb_ref,
                    qg_ref, kvg_ref, wuq_ref, wuk_ref, wuvt_ref, cos_ref, sin_ref,
                    ya_ref, bg_ref, q_ref, k_ref, vt_ref, ubuf):
    tm = x_ref.shape[1]

    @pl.when(pl.program_id(1) == 0)
    def _():
        ubuf[0:SUBLANES, :] = jnp.zeros((SUBLANES, SC_WIDTH), F32)

    x = x_ref[0]
    h = _rms(x, png_ref[...]) * (1.0 + scale_ref[0]) + shift_ref[0]
    hb = h.astype(BF16)

    def proj(off, width):
        return jnp.dot(hb, win_ref[:, off:off + width], preferred_element_type=F32)

    cos_t = cos_ref[0]
    sin_t = sin_ref[0]
    lane = lax.broadcasted_iota(jnp.int32, (tm, LANES), 1)
    is_t2 = lane >= ROPE_LO + ROPE_HALF

    def rope(t):
        return t * cos_t + _swap_halves(t, is_t2) * sin_t

    z_kv = proj(OFF_CKV, KV_LORA + HEAD_PAD)
    ckv = _rms(z_kv[:, 0:KV_LORA], kvg_ref[...]).astype(BF16)
    vt = lax.dot_general(wuvt_ref[...], ckv, (((1,), (1,)), ((), ())),
                         preferred_element_type=F32).astype(BF16)
    ones_rows = (lax.broadcasted_iota(jnp.int32, (VT_ROWS - V_HEAD, tm), 0) == 0).astype(BF16)
    vt_aug = jnp.concatenate(
        [part for hd in range(MLA_HEADS) for part in (vt[hd * V_HEAD:(hd + 1) * V_HEAD], ones_rows)],
        axis=0)
    for j in range(tm // TK):
        vt_ref[0, j] = vt_aug[:, j * TK:(j + 1) * TK]
    kn = jnp.dot(ckv, wuk_ref[...], preferred_element_type=F32)
    kr = rope(z_kv[:, KV_LORA:KV_LORA + HEAD_PAD])
    for hd in range(MLA_HEADS):
        sl = slice(hd * HEAD_PAD, (hd + 1) * HEAD_PAD)
        k_ref[0, :, sl] = (kn[:, sl] + kr).astype(BF16)

    cq = _rms(proj(OFF_CQ, Q_LORA), qg_ref[...]).astype(BF16)
    qf = jnp.dot(cq, wuq_ref[...], preferred_element_type=F32)
    q_scale = math.log2(math.e) / math.sqrt(QK_NOPE + QK_ROPE)
    for hd in range(MLA_HEADS):
        sl = slice(hd * HEAD_PAD, (hd + 1) * HEAD_PAD)
        q_ref[0, :, sl] = (rope(qf[:, sl]) * q_scale).astype(BF16)

    u = proj(OFF_AC, SC_WIDTH) * proj(OFF_AX, SC_WIDTH)
    ubuf[SUBLANES:SUBLANES + tm, :] = u
    conv = (scw_ref[2:3, :] * u
            + scw_ref[1:2, :] * ubuf[SUBLANES - 1:SUBLANES - 1 + tm, :]
            + scw_ref[0:1, :] * ubuf[SUBLANES - 2:SUBLANES - 2 + tm, :]
            + scb_ref[...])
    ubuf[0:SUBLANES, :] = ubuf[tm:tm + SUBLANES, :]
    y_a = proj(OFF_AB, SC_WIDTH) * conv * _silu(proj(OFF_AG, SC_WIDTH))
    ya_ref[0] = y_a.astype(BF16)
    bg_ref[0] = _silu(proj(OFF_BG, MLA_WIDTH)).astype(BF16)


def _even_in_call(x, mod_l, png, win, scw, scb, qg, kvg, wuq, wuk, wuvt, cos_t, sin_t):
    b, s, d = x.shape
    hp = MLA_HEADS * HEAD_PAD
    tile = lambda w: pl.BlockSpec((1, TM, w), lambda i, j: (i, j, 0))
    whole = lambda a: pl.BlockSpec(a.shape, lambda i, j: (0,) * a.ndim)
    modspec = lambda k: pl.BlockSpec((1, 1, d), lambda i, j: (i, 0, k))
    return pl.pallas_call(
        _even_in_kernel,
        grid=(b, s // TM),
        in_specs=[tile(d), modspec(0), modspec(1), whole(png), whole(win), whole(scw), whole(scb),
                  whole(qg), whole(kvg), whole(wuq), whole(wuk), whole(wuvt), tile(LANES), tile(LANES)],
        out_specs=[tile(SC_WIDTH), tile(MLA_WIDTH), tile(hp), tile(hp),
                   pl.BlockSpec((1, TM // TK, MLA_HEADS * VT_ROWS, TK), lambda i, j: (i, j, 0, 0))],
        out_shape=[jax.ShapeDtypeStruct((b, s, SC_WIDTH), BF16),
                   jax.ShapeDtypeStruct((b, s, MLA_WIDTH), BF16),
                   jax.ShapeDtypeStruct((b, s, hp), BF16),
                   jax.ShapeDtypeStruct((b, s, hp), BF16),
                   jax.ShapeDtypeStruct((b, s // TK, MLA_HEADS * VT_ROWS, TK), BF16)],
        scratch_shapes=[pltpu.VMEM((TM + SUBLANES, SC_WIDTH), F32)],
        compiler_params=pltpu.CompilerParams(dimension_semantics=("arbitrary", "arbitrary"),
                                             vmem_limit_bytes=VMEM_LIMIT),
    )(x, mod_l, mod_l, png, win, scw, scb, qg, kvg, wuq, wuk, wuvt, cos_t, sin_t)


def _attn_out_kernel(q_ref, k_ref, vt_ref, bg_ref, x_ref, ya_ref, gate_ref, pog_ref, wout_ref,
                     o_ref, m_s, acc_s, yb_s, s_pre, ya_s):
    qi = pl.program_id(1)
    m_s[...] = jnp.full(m_s.shape, NEG_BIG, F32)
    acc_s[...] = jnp.zeros(acc_s.shape, F32)

    key_chunk = lax.broadcasted_iota(jnp.int32, (TK, TQ), 0) // CHUNK
    qry_chunk = lax.broadcasted_iota(jnp.int32, (TK, TQ), 1) // CHUNK

    def scores(kb, hd):
        rows = pl.ds(pl.multiple_of(kb * TK, TK), TK)
        sl = slice(hd * HEAD_PAD, (hd + 1) * HEAD_PAD)
        return lax.dot_general(k_ref[0, rows, sl], q_ref[0, :, sl], (((1,), (1,)), ((), ())),
                               preferred_element_type=F32)

    def block(kb, diag=None, last=False):
        masked = diag is not None

        def update(hd, s):
            if masked:
                s = jnp.where(key_chunk + diag * (TK // CHUNK) <= qry_chunk, s, NEG_BIG)
            m_prev = m_s[hd]
            m_new = jnp.maximum(m_prev, jnp.max(s, axis=0, keepdims=True))
            alpha = jnp.exp2(m_prev - m_new)
            p = jnp.exp2(s - m_new)
            pv = jnp.dot(vt_ref[0, kb, hd * VT_ROWS:(hd + 1) * VT_ROWS, :], p.astype(BF16),
                         preferred_element_type=F32)
            acc_s[hd] = alpha * acc_s[hd] + pv
            m_s[hd] = m_new

        pending = {hd: s_pre[hd] for hd in range(QK_PARKED)}
        for hd in range(QK_PARKED, QK_AHEAD):
            pending[hd] = scores(kb, hd)
        for hd in range(MLA_HEADS):
            ahead = hd + QK_AHEAD
            if ahead < MLA_HEADS:
                pending[ahead] = scores(kb, ahead)
            elif not last and ahead - MLA_HEADS < QK_PARKED:
                s_pre[ahead - MLA_HEADS] = scores(kb + 1, ahead - MLA_HEADS)
            elif last:
                width = D_MODEL // QK_AHEAD
                cols = slice((ahead - MLA_HEADS) * width, (ahead - MLA_HEADS + 1) * width)
                ya_s[:, cols] = jnp.dot(ya_ref[0], wout_ref[0:SC_WIDTH, cols],
                                        preferred_element_type=F32)
            update(hd, pending.pop(hd))

    def pair_body(kk, carry):
        block(2 * kk)
        block(2 * kk + 1)
        return carry

    def single_body(kb, carry):
        block(kb)
        return carry

    span = TQ // TK
    visible = qi * span
    for hd in range(QK_PARKED):
        s_pre[hd] = scores(0, hd)
    lax.fori_loop(0, lax.shift_right_logical(visible, 1), pair_body, 0)
    lax.fori_loop(visible - (visible & 1), visible, single_body, 0)
    for d in range(span):
        block(qi * span + d, diag=d, last=(d == span - 1))

    for pair in range(MLA_HEADS // 2):
        both = jnp.concatenate(
            [acc_s[hd, 0:V_HEAD, :] * (1.0 / acc_s[hd, V_HEAD:V_HEAD + 1, :])
             for hd in (2 * pair, 2 * pair + 1)], axis=0)
        cols = slice(pair * LANES, (pair + 1) * LANES)
        yb_s[:, cols] = (both.T * bg_ref[0, :, cols].astype(F32)).astype(BF16)
    y = ya_s[...] + jnp.dot(yb_s[...], wout_ref[SC_WIDTH:SC_WIDTH + MLA_WIDTH, :],
                            preferred_element_type=F32)
    o_ref[0] = x_ref[0] + gate_ref[0] * _rms(y, pog_ref[...])


def _attn_out_call(q, k, vt, bg, x, ya, mod_l, pog, wout):
    b, s, d = x.shape
    hp = MLA_HEADS * HEAD_PAD
    tile = lambda w: pl.BlockSpec((1, TQ, w), lambda i, j: (i, j, 0))
    whole = lambda a: pl.BlockSpec(a.shape, lambda i, j: (0,) * a.ndim)
    return pl.pallas_call(
        _attn_out_kernel,
        grid=(b, s // TQ),
        in_specs=[tile(hp),
                  pl.BlockSpec((1, s, hp), lambda i, j: (i, 0, 0)),
                  pl.BlockSpec((1, s // TK, MLA_HEADS * VT_ROWS, TK), lambda i, j: (i, 0, 0, 0)),
                  tile(MLA_WIDTH), tile(d), tile(SC_WIDTH),
                  pl.BlockSpec((1, 1, d), lambda i, j: (i, 0, 2)), whole(pog), whole(wout)],
        out_specs=tile(d),
        out_shape=jax.ShapeDtypeStruct((b, s, d), F32),
        scratch_shapes=[pltpu.VMEM((MLA_HEADS, 1, TQ), F32),
                        pltpu.VMEM((MLA_HEADS, VT_ROWS, TQ), F32),
                        pltpu.VMEM((TQ, MLA_WIDTH), BF16),
                        pltpu.VMEM((QK_PARKED, TK, TQ), F32),
                        pltpu.VMEM((TQ, D_MODEL), F32)],
        compiler_params=pltpu.CompilerParams(dimension_semantics=("arbitrary", "arbitrary"),
                                             vmem_limit_bytes=VMEM_LIMIT),
    )(q, k, vt, bg, x, ya, mod_l, pog, wout)


def _odd_kernel(x_ref, shift_ref, scale_ref, gate_ref, png_ref, pog_ref, win_ref, cw_ref, cb_ref,
                lg_ref, lb_ref, wout_ref, o_ref, ubuf, vbuf, gbuf):
    tm = x_ref.shape[1]
    cw = CONF_WIDTH

    @pl.when(pl.program_id(1) == 0)
    def _():
        ubuf[:, 0:CONF_HALO, :] = jnp.zeros((cw // LANES, CONF_HALO, LANES), F32)

    half = tm // 2
    n_slab = cw // LANES
    hbs = []
    for h in range(2):
        xh = x_ref[0, h * half:(h + 1) * half, :]
        hbs.append((_rms(xh, png_ref[...]) * (1.0 + scale_ref[0]) + shift_ref[0]).astype(BF16))

    def gate_inputs(h, c0):
        rows = slice(h * half, (h + 1) * half)

        def proj(off):
            return jnp.dot(hbs[h], win_ref[:, off:off + CONF_GROUP], preferred_element_type=F32)

        u = proj(c0) * jax.nn.sigmoid(proj(cw + c0))
        gbuf[rows, c0:c0 + CONF_GROUP] = _silu(proj(2 * cw + c0))
        for n in range(CONF_GROUP // LANES):
            ubuf[c0 // LANES + n, CONF_HALO + h * half:CONF_HALO + (h + 1) * half, :] = (
                u[:, n * LANES:(n + 1) * LANES])

    def row_groups(r0):
        groups, g0 = [], r0
        while g0 < r0 + half:
            seg = CONV_SEG if r0 + half - g0 >= SUBLANES * CONV_SEG else CONV_SEG_TAIL
            groups.append((g0, seg))
            g0 += SUBLANES * seg
        return groups

    def conv(h, c0):
        for n in range(CONF_GROUP // LANES):
            slab = c0 // LANES + n
            cs = slice(slab * LANES, (slab + 1) * LANES)
            for g0, seg in row_groups(h * half):
                taps = {m: ubuf[slab, pl.ds(CONF_HALO + g0 + m, SUBLANES, stride=seg), :]
                        for m in range(1 - CONF_KERNEL, seg)}
                acc = [jnp.broadcast_to(cb_ref[:, cs], (SUBLANES, LANES)) for _ in range(seg)]
                for j in range(CONF_KERNEL):
                    w_j = jnp.broadcast_to(cw_ref[CONF_KERNEL - 1 - j:CONF_KERNEL - j, cs], (SUBLANES, LANES))
                    for i in range(seg):
                        acc[i] = acc[i] + w_j * taps[i - j]
                for i in range(seg):
                    vbuf[slab, pl.ds(g0 + i, SUBLANES, stride=seg), :] = acc[i]

    channel_groups = list(range(0, cw, CONF_GROUP))
    gate_inputs(0, channel_groups[0])
    for n, c0 in enumerate(channel_groups):
        conv(0, c0)
        gate_inputs(1, c0)
        conv(1, c0)
        if n + 1 < len(channel_groups):
            gate_inputs(0, channel_groups[n + 1])
    ubuf[:, 0:CONF_HALO, :] = ubuf[:, tm:tm + CONF_HALO, :]

    def norm_project(h):
        rows = slice(h * half, (h + 1) * half)
        mu = sum(jnp.sum(vbuf[c, rows, :], axis=-1, keepdims=True) for c in range(n_slab)) * (1.0 / cw)
        var = sum(jnp.sum(jnp.square(vbuf[c, rows, :] - mu), axis=-1, keepdims=True)
                  for c in range(n_slab)) * (1.0 / cw)
        rstd = lax.rsqrt(var + EPS)
        y = jnp.zeros((half, cw), F32)
        for c0 in channel_groups:
            cs = slice(c0, c0 + CONF_GROUP)
            v = jnp.concatenate([vbuf[c0 // LANES + n, rows, :] for n in range(CONF_GROUP // LANES)], axis=-1)
            ln = (v - mu) * rstd * lg_ref[:, cs] + lb_ref[:, cs]
            t = _silu(ln) * gbuf[rows, cs]
            y = y + jnp.dot(t.astype(BF16), wout_ref[cs, :], preferred_element_type=F32)
        return y

    ys = [norm_project(h) for h in range(2)]
    for h in range(2):
        rows = slice(h * half, (h + 1) * half)
        o_ref[0, rows, :] = x_ref[0, rows, :] + gate_ref[0] * _rms(ys[h], pog_ref[...])


def _odd_call(x, mod_l, png, pog, win, cw, cb, lg, lb, wout):
    b, s, d = x.shape
    tile = pl.BlockSpec((1, TM, d), lambda i, j: (i, j, 0))
    whole = lambda a: pl.BlockSpec(a.shape, lambda i, j: (0,) * a.ndim)
    modspec = lambda k: pl.BlockSpec((1, 1, d), lambda i, j: (i, 0, k))
    return pl.pallas_call(
        _odd_kernel,
        grid=(b, s // TM),
        in_specs=[tile, modspec(0), modspec(1), modspec(2), whole(png), whole(pog), whole(win),
                  whole(cw), whole(cb), whole(lg), whole(lb), whole(wout)],
        out_specs=tile,
        out_shape=jax.ShapeDtypeStruct((b, s, d), F32),
        scratch_shapes=[pltpu.VMEM((CONF_WIDTH // LANES, TM + CONF_HALO, LANES), F32),
                        pltpu.VMEM((CONF_WIDTH // LANES, TM, LANES), F32),
                        pltpu.VMEM((TM, CONF_WIDTH), F32)],
        compiler_params=pltpu.CompilerParams(dimension_semantics=("arbitrary", "arbitrary"),
                                             vmem_limit_bytes=VMEM_LIMIT),
    )(x, mod_l, mod_l, mod_l, png, pog, win, cw, cb, lg, lb, wout)


def _pack_even_w_in(w):
    d = w.shape[0]
    w = w.astype(BF16)
    a = w[:, :4 * SC_WIDTH]
    o = 4 * SC_WIDTH
    c_q = w[:, o:o + Q_LORA]
    c_kv = w[:, o + Q_LORA:o + Q_LORA + KV_LORA]
    k_r = w[:, o + Q_LORA + KV_LORA:o + Q_LORA + KV_LORA + QK_ROPE]
    b_g = w[:, o + Q_LORA + KV_LORA + QK_ROPE:]
    k_r = jnp.concatenate([jnp.zeros((d, ROPE_LO), w.dtype), k_r,
                           jnp.zeros((d, HEAD_PAD - ROPE_LO - QK_ROPE), w.dtype)], axis=1)
    return jnp.concatenate([a, b_g, c_q, c_kv, k_r], axis=1)


def _pad_heads(w, width):
    k = w.shape[0]
    w = w.astype(BF16).reshape(k, MLA_HEADS, width)
    w = jnp.pad(w, ((0, 0), (0, 0), (0, HEAD_PAD - width)))
    return w.reshape(k, MLA_HEADS * HEAD_PAD)


def kernel(x, c, positions, ada_w, ada_b, pre_norm_g, post_norm_g, even_w_in, even_sc_conv_w, even_sc_conv_b, even_q_norm_g, even_kv_norm_g, even_w_uq, even_w_ukv, even_w_out, odd_w_in, odd_conv_w, odd_conv_b, odd_ln_g, odd_ln_b, odd_w_out):
    b, s, d = x.shape
    assert d == D_MODEL and s % TM == 0 and s % TQ == 0 and TM % TK == 0 and b <= SUBLANES
    assert TQ % CHUNK == 0 and TK % CHUNK == 0 and TQ % TK == 0
    assert CONV_SEG % SUBLANES and CONV_SEG_TAIL % SUBLANES
    assert ((TM // 2) % (SUBLANES * CONV_SEG)) % (SUBLANES * CONV_SEG_TAIL) == 0

    c_pad = jnp.pad(c, ((0, SUBLANES - b), (0, 0)))
    mod = _mod_call(c_pad, ada_w, ada_b.reshape(DEPTH, 1, 3 * d))

    inv_freq = 1.0 / (ROPE_THETA ** (jnp.arange(0, QK_ROPE, 2, dtype=F32) / QK_ROPE))
    freq_row = jnp.concatenate([jnp.zeros((ROPE_LO,), F32), inv_freq, inv_freq,
                                jnp.zeros((HEAD_PAD - ROPE_LO - QK_ROPE,), F32)]).reshape(1, LANES)
    cos_t, sin_t = _rope_call(positions.reshape(b, s, 1), freq_row)

    row = lambda a: a.reshape(1, -1)
    for layer in range(DEPTH):
        mod_l = mod[layer, :b].reshape(b, 1, 3 * d)
        png, pog = row(pre_norm_g[layer]), row(post_norm_g[layer])
        i = layer // 2
        if layer % 2 == 0:
            w_ukv = even_w_ukv[i].reshape(KV_LORA, MLA_HEADS, QK_NOPE + V_HEAD)
            wuk = _pad_heads(w_ukv[:, :, :QK_NOPE].reshape(KV_LORA, -1), QK_NOPE)
            wuvt = w_ukv[:, :, QK_NOPE:].reshape(KV_LORA, MLA_WIDTH).T.astype(BF16)
            wuq = _pad_heads(even_w_uq[i], QK_NOPE + QK_ROPE)
            ya, bg, q, k, vt = _even_in_call(
                x, mod_l, png, _pack_even_w_in(even_w_in[i]), even_sc_conv_w[i],
                row(even_sc_conv_b[i]), row(even_q_norm_g[i]), row(even_kv_norm_g[i]),
                wuq, wuk, wuvt, cos_t, sin_t)
            x = _attn_out_call(q, k, vt, bg, x, ya, mod_l, pog, even_w_out[i].astype(BF16))
        else:
            x = _odd_call(x, mod_l, png, pog, odd_w_in[i].astype(BF16), odd_conv_w[i],
                          row(odd_conv_b[i]), row(odd_ln_g[i]), row(odd_ln_b[i]),
                          odd_w_out[i].astype(BF16))
    return x
```

```python
import math

import jax
import jax.numpy as jnp
from jax import lax
from jax.experimental import pallas as pl
from jax.experimental.pallas import tpu as pltpu

D_MODEL = 1024
DEPTH = 4
CHUNK = 64
SC_WIDTH = 512
SC_KERNEL = 3
MLA_HEADS = 8
QK_NOPE = 64
QK_ROPE = 32
V_HEAD = 64
Q_LORA = 256
KV_LORA = 128
MLA_WIDTH = MLA_HEADS * V_HEAD
VT_ROWS = V_HEAD + 16
ROPE_THETA = 10000.0
CONF_WIDTH = D_MODEL
CONF_KERNEL = 31
EPS = 1e-6

LANES = 128
HEAD_PAD = LANES
ROPE_LO = QK_NOPE
ROPE_HALF = QK_ROPE // 2
SUBLANES = 8
CONF_HALO = 32

OFF_AB, OFF_AC, OFF_AX, OFF_AG, OFF_BG = 0, 512, 1024, 1536, 2048
OFF_CQ = 2560
OFF_CKV = OFF_CQ + Q_LORA
OFF_KR = OFF_CKV + KV_LORA
EVEN_IN_PAD = OFF_KR + HEAD_PAD

TM = 512
CONF_GROUP = 256
TQ = 256
TK = 256
CONV_SEG_TAIL = 4
CONV_SEG = 12
QK_PARKED = 4
QK_AHEAD = 4
MOD_SPLIT = 4
NEG_BIG = -1e30
VMEM_LIMIT = 56 * 1024 * 1024

F32 = jnp.float32
BF16 = jnp.bfloat16


def _rms(x, g):
    return x * lax.rsqrt(jnp.mean(x * x, axis=-1, keepdims=True) + EPS) * g


def _silu(x):
    return x * jax.nn.sigmoid(x)


def _swap_halves(t, is_t2):
    return jnp.where(is_t2, pltpu.roll(t, ROPE_HALF, 1), pltpu.roll(t, LANES - ROPE_HALF, 1))


def _mod_kernel(c_ref, *refs):
    w_refs, b_ref, o_ref = refs[:MOD_SPLIT], refs[MOD_SPLIT], refs[MOD_SPLIT + 1]
    a = _silu(c_ref[...]).astype(BF16)
    rows = D_MODEL // MOD_SPLIT
    acc = b_ref[0]
    for n, w_ref in enumerate(w_refs):
        acc = acc + jnp.dot(a[:, n * rows:(n + 1) * rows], w_ref[0].astype(BF16),
                            preferred_element_type=F32)
    o_ref[0] = acc


def _mod_call(c_pad, ada_w, ada_b3):
    d = D_MODEL
    rows = d // MOD_SPLIT
    band = lambda n: pl.BlockSpec((1, rows, d), lambda l, j: (l, n, j))
    return pl.pallas_call(
        _mod_kernel,
        grid=(DEPTH, 3),
        in_specs=[pl.BlockSpec((SUBLANES, d), lambda l, j: (0, 0))]
                 + [band(n) for n in range(MOD_SPLIT)]
                 + [pl.BlockSpec((1, 1, d), lambda l, j: (l, 0, j))],
        out_specs=pl.BlockSpec((1, SUBLANES, d), lambda l, j: (l, 0, j)),
        out_shape=jax.ShapeDtypeStruct((DEPTH, SUBLANES, 3 * d), F32),
        compiler_params=pltpu.CompilerParams(dimension_semantics=("arbitrary", "arbitrary")),
    )(c_pad, *([ada_w] * MOD_SPLIT), ada_b3)


def _rope_kernel(pos_ref, freq_ref, cos_ref, sin_ref):
    ang = pos_ref[0].astype(F32) * freq_ref[...]
    lane = lax.broadcasted_iota(jnp.int32, ang.shape, 1)
    is_t1 = (lane >= ROPE_LO) & (lane < ROPE_LO + ROPE_HALF)
    is_t2 = (lane >= ROPE_LO + ROPE_HALF) & (lane < ROPE_LO + QK_ROPE)
    c = jnp.cos(ang)
    s = jnp.sin(ang)
    cos_ref[0] = jnp.where(is_t1 | is_t2, c, 1.0)
    sin_ref[0] = jnp.where(is_t1, -s, jnp.where(is_t2, s, 0.0))


def _rope_call(pos3, freq_row):
    b, s, _ = pos3.shape
    ts = 512
    spec = pl.BlockSpec((1, ts, LANES), lambda i, j: (i, j, 0))
    return pl.pallas_call(
        _rope_kernel,
        grid=(b, s // ts),
        in_specs=[
            pl.BlockSpec((1, ts, 1), lambda i, j: (i, j, 0)),
            pl.BlockSpec((1, LANES), lambda i, j: (0, 0)),
        ],
        out_specs=[spec, spec],
        out_shape=[jax.ShapeDtypeStruct((b, s, LANES), F32)] * 2,
        compiler_params=pltpu.CompilerParams(dimension_semantics=("arbitrary", "arbitrary")),
    )(pos3, freq_row)


def _even_in_kernel(x_ref, shift_ref, scale_ref, png_ref, win_ref, scw_ref, scb_ref,
                    qg_ref, kvg_ref, wuq_ref, wuk_ref, wuvt_ref, cos_ref, sin_ref,
                    ya_ref, bg_ref, q_ref, k_ref, vt_ref, ubuf):
    tm = x_ref.shape[1]

    @pl.when(pl.program_id(1) == 0)
    def _():
        ubuf[0:SUBLANES, :] = jnp.zeros((SUBLANES, SC_WIDTH), F32)

    x = x_ref[0]
    h = _rms(x, png_ref[...]) * (1.0 + scale_ref[0]) + shift_ref[0]
    hb = h.astype(BF16)

    def proj(off, width):
        return jnp.dot(hb, win_ref[:, off:off + width], preferred_element_type=F32)

    cos_t = cos_ref[0]
    sin_t = sin_ref[0]
    lane = lax.broadcasted_iota(jnp.int32, (tm, LANES), 1)
    is_t2 = lane >= ROPE_LO + ROPE_HALF

    def rope(t):
        return t * cos_t + _swap_halves(t, is_t2) * sin_t

    z_kv = proj(OFF_CKV, KV_LORA + HEAD_PAD)
    ckv = _rms(z_kv[:, 0:KV_LORA], kvg_ref[...]).astype(BF16)
    vt = lax.dot_general(wuvt_ref[...], ckv, (((1,), (1,)), ((), ())),
                         preferred_element_type=F32).astype(BF16)
    ones_rows = (lax.broadcasted_iota(jnp.int32, (VT_ROWS - V_HEAD, tm), 0) == 0).astype(BF16)
    vt_aug = jnp.concatenate(
        [part for hd in range(MLA_HEADS) for part in (vt[hd * V_HEAD:(hd + 1) * V_HEAD], ones_rows)],
        axis=0)
    for j in range(tm // TK):
        vt_ref[0, j] = vt_aug[:, j * TK:(j + 1) * TK]
    kn = jnp.dot(ckv, wuk_ref[...], preferred_element_type=F32)
    kr = rope(z_kv[:, KV_LORA:KV_LORA + HEAD_PAD])
    for hd in range(MLA_HEADS):
        sl = slice(hd * HEAD_PAD, (hd + 1) * HEAD_PAD)
        k_ref[0, :, sl] = (kn[:, sl] + kr).astype(BF16)

    cq = _rms(proj(OFF_CQ, Q_LORA), qg_ref[...]).astype(BF16)
    qf = jnp.dot(cq, wuq_ref[...], preferred_element_type=F32)
    q_scale = math.log2(math.e) / math.sqrt(QK_NOPE + QK_ROPE)
    for hd in range(MLA_HEADS):
        sl = slice(hd * HEAD_PAD, (hd + 1) * HEAD_PAD)
        q_ref[0, :, sl] = (rope(qf[:, sl]) * q_scale).astype(BF16)

    u = proj(OFF_AC, SC_WIDTH) * proj(OFF_AX, SC_WIDTH)
    ubuf[SUBLANES:SUBLANES + tm, :] = u
    conv = (scw_ref[2:3, :] * u
            + scw_ref[1:2, :] * ubuf[SUBLANES - 1:SUBLANES - 1 + tm, :]
            + scw_ref[0:1, :] * ubuf[SUBLANES - 2:SUBLANES - 2 + tm, :]
            + scb_ref[...])
    ubuf[0:SUBLANES, :] = ubuf[tm:tm + SUBLANES, :]
    y_a = proj(OFF_AB, SC_WIDTH) * conv * _silu(proj(OFF_AG, SC_WIDTH))
    ya_ref[0] = y_a.astype(BF16)
    bg_ref[0] = _silu(proj(OFF_BG, MLA_WIDTH)).astype(BF16)


def _even_in_call(x, mod_l, png, win, scw, scb, qg, kvg, wuq, wuk, wuvt, cos_t, sin_t):
    b, s, d = x.shape
    hp = MLA_HEADS * HEAD_PAD
    tile = lambda w: pl.BlockSpec((1, TM, w), lambda i, j: (i, j, 0))
    whole = lambda a: pl.BlockSpec(a.shape, lambda i, j: (0,) * a.ndim)
    modspec = lambda k: pl.BlockSpec((1, 1, d), lambda i, j: (i, 0, k))
    return pl.pallas_call(
        _even_in_kernel,
        grid=(b, s // TM),
        in_specs=[tile(d), modspec(0), modspec(1), whole(png), whole(win), whole(scw), whole(scb),
                  whole(qg), whole(kvg), whole(wuq), whole(wuk), whole(wuvt), tile(LANES), tile(LANES)],
        out_specs=[tile(SC_WIDTH), tile(MLA_WIDTH), tile(hp), tile(hp),
                   pl.BlockSpec((1, TM // TK, MLA_HEADS * VT_ROWS, TK), lambda i, j: (i, j, 0, 0))],
        out_shape=[jax.ShapeDtypeStruct((b, s, SC_WIDTH), BF16),
                   jax.ShapeDtypeStruct((b, s, MLA_WIDTH), BF16),
                   jax.ShapeDtypeStruct((b, s, hp), BF16),
                   jax.ShapeDtypeStruct((b, s, hp), BF16),
                   jax.ShapeDtypeStruct((b, s // TK, MLA_HEADS * VT_ROWS, TK), BF16)],
        scratch_shapes=[pltpu.VMEM((TM + SUBLANES, SC_WIDTH), F32)],
        compiler_params=pltpu.CompilerParams(dimension_semantics=("arbitrary", "arbitrary"),
                                             vmem_limit_bytes=VMEM_LIMIT),
    )(x, mod_l, mod_l, png, win, scw, scb, qg, kvg, wuq, wuk, wuvt, cos_t, sin_t)


def _attn_out_kernel(q_ref, k_ref, vt_ref, bg_ref, x_ref, ya_ref, gate_ref, pog_ref, wout_ref,
                     o_ref, m_s, acc_s, yb_s, s_pre, ya_s):
    qi = pl.program_id(1)
    m_s[...] = jnp.full(m_s.shape, NEG_BIG, F32)
    acc_s[...] = jnp.zeros(acc_s.shape, F32)

    key_chunk = lax.broadcasted_iota(jnp.int32, (TK, TQ), 0) // CHUNK
    qry_chunk = lax.broadcasted_iota(jnp.int32, (TK, TQ), 1) // CHUNK

    def scores(kb, hd):
        rows = pl.ds(pl.multiple_of(kb * TK, TK), TK)
        sl = slice(hd * HEAD_PAD, (hd + 1) * HEAD_PAD)
        return lax.dot_general(k_ref[0, rows, sl], q_ref[0, :, sl], (((1,), (1,)), ((), ())),
                               preferred_element_type=F32)

    def block(kb, diag=None, last=False):
        masked = diag is not None

        def update(hd, s):
            if masked:
                s = jnp.where(key_chunk + diag * (TK // CHUNK) <= qry_chunk, s, NEG_BIG)
            m_prev = m_s[hd]
            m_new = jnp.maximum(m_prev, jnp.max(s, axis=0, keepdims=True))
            alpha = jnp.exp2(m_prev - m_new)
            p = jnp.exp2(s - m_new)
            pv = jnp.dot(vt_ref[0, kb, hd * VT_ROWS:(hd + 1) * VT_ROWS, :], p.astype(BF16),
                         preferred_element_type=F32)
            acc_s[hd] = alpha * acc_s[hd] + pv
            m_s[hd] = m_new

        pending = {hd: s_pre[hd] for hd in range(QK_PARKED)}
        for hd in range(QK_PARKED, QK_AHEAD):
            pending[hd] = scores(kb, hd)
        for hd in range(MLA_HEADS):
            ahead = hd + QK_AHEAD
            if ahead < MLA_HEADS:
                pending[ahead] = scores(kb, ahead)
            elif not last and ahead - MLA_HEADS < QK_PARKED:
                s_pre[ahead - MLA_HEADS] = scores(kb + 1, ahead - MLA_HEADS)
            elif last:
                width = D_MODEL // QK_AHEAD
                cols = slice((ahead - MLA_HEADS) * width, (ahead - MLA_HEADS + 1) * width)
                ya_s[:, cols] = jnp.dot(ya_ref[0], wout_ref[0:SC_WIDTH, cols],
                                        preferred_element_type=F32)
            update(hd, pending.pop(hd))

    def pair_body(kk, carry):
        block(2 * kk)
        block(2 * kk + 1)
        return carry

    def single_body(kb, carry):
        block(kb)
        return carry

    span = TQ // TK
    visible = qi * span
    for hd in range(QK_PARKED):
        s_pre[hd] = scores(0, hd)
    lax.fori_loop(0, lax.shift_right_logical(visible, 1), pair_body, 0)
    lax.fori_loop(visible - (visible & 1), visible, single_body, 0)
    for d in range(span):
        block(qi * span + d, diag=d, last=(d == span - 1))

    for pair in range(MLA_HEADS // 2):
        both = jnp.concatenate(
            [acc_s[hd, 0:V_HEAD, :] * (1.0 / acc_s[hd, V_HEAD:V_HEAD + 1, :])
             for hd in (2 * pair, 2 * pair + 1)], axis=0)
        cols = slice(pair * LANES, (pair + 1) * LANES)
        yb_s[:, cols] = (both.T * bg_ref[0, :, cols].astype(F32)).astype(BF16)
    y = ya_s[...] + jnp.dot(yb_s[...], wout_ref[SC_WIDTH:SC_WIDTH + MLA_WIDTH, :],
                            preferred_element_type=F32)
    o_ref[0] = x_ref[0] + gate_ref[0] * _rms(y, pog_ref[...])


def _attn_out_call(q, k, vt, bg, x, ya, mod_l, pog, wout):
    b, s, d = x.shape
    hp = MLA_HEADS * HEAD_PAD
    tile = lambda w: pl.BlockSpec((1, TQ, w), lambda i, j: (i, j, 0))
    whole = lambda a: pl.BlockSpec(a.shape, lambda i, j: (0,) * a.ndim)
    return pl.pallas_call(
        _attn_out_kernel,
        grid=(b, s // TQ),
        in_specs=[tile(hp),
                  pl.BlockSpec((1, s, hp), lambda i, j: (i, 0, 0)),
                  pl.BlockSpec((1, s // TK, MLA_HEADS * VT_ROWS, TK), lambda i, j: (i, 0, 0, 0)),
                  tile(MLA_WIDTH), tile(d), tile(SC_WIDTH),
                  pl.BlockSpec((1, 1, d), lambda i, j: (i, 0, 2)), whole(pog), whole(wout)],
        out_specs=tile(d),
        out_shape=jax.ShapeDtypeStruct((b, s, d), F32),
        scratch_shapes=[pltpu.VMEM((MLA_HEADS, 1, TQ), F32),
                        pltpu.VMEM((MLA_HEADS, VT_ROWS, TQ), F32),
                        pltpu.VMEM((TQ, MLA_WIDTH), BF16),
                        pltpu.VMEM((QK_PARKED, TK, TQ), F32),
                        pltpu.VMEM((TQ, D_MODEL), F32)],
        compiler_params=pltpu.CompilerParams(dimension_semantics=("arbitrary", "arbitrary"),
                                             vmem_limit_bytes=VMEM_LIMIT),
    )(q, k, vt, bg, x, ya, mod_l, pog, wout)


def _odd_kernel(x_ref, shift_ref, scale_ref, gate_ref, png_ref, pog_ref, win_ref, cw_ref, cb_ref,
                lg_ref, lb_ref, wout_ref, o_ref, ubuf, vbuf, gbuf):
    tm = x_ref.shape[1]
    cw = CONF_WIDTH

    @pl.when(pl.program_id(1) == 0)
    def _():
        ubuf[:, 0:CONF_HALO, :] = jnp.zeros((cw // LANES, CONF_HALO, LANES), F32)

    x = x_ref[0]
    hb = (_rms(x, png_ref[...]) * (1.0 + scale_ref[0]) + shift_ref[0]).astype(BF16)

    def proj(off):
        return jnp.dot(hb, win_ref[:, off:off + CONF_GROUP], preferred_element_type=F32)

    groups, g0 = [], 0
    while g0 < tm:
        seg = CONV_SEG if tm - g0 >= SUBLANES * CONV_SEG else CONV_SEG_TAIL
        groups.append((g0, seg))
        g0 += SUBLANES * seg

    for c0 in range(0, cw, CONF_GROUP):
        u = proj(c0) * jax.nn.sigmoid(proj(cw + c0))
        gbuf[:, c0:c0 + CONF_GROUP] = _silu(proj(2 * cw + c0))
        for n in range(CONF_GROUP // LANES):
            slab = c0 // LANES + n
            cs = slice(slab * LANES, (slab + 1) * LANES)
            ubuf[slab, CONF_HALO:CONF_HALO + tm, :] = u[:, n * LANES:(n + 1) * LANES]
            for g0, seg in groups:
                taps = {m: ubuf[slab, pl.ds(CONF_HALO + g0 + m, SUBLANES, stride=seg), :]
                        for m in range(1 - CONF_KERNEL, seg)}
                acc = [jnp.broadcast_to(cb_ref[:, cs], (SUBLANES, LANES)) for _ in range(seg)]
                for j in range(CONF_KERNEL):
                    w_j = jnp.broadcast_to(cw_ref[CONF_KERNEL - 1 - j:CONF_KERNEL - j, cs], (SUBLANES, LANES))
                    for i in range(seg):
                        acc[i] = acc[i] + w_j * taps[i - j]
                for i in range(seg):
                    vbuf[slab, pl.ds(g0 + i, SUBLANES, stride=seg), :] = acc[i]
    ubuf[:, 0:CONF_HALO, :] = ubuf[:, tm:tm + CONF_HALO, :]

    n_slab = cw // LANES
    mu = sum(jnp.sum(vbuf[c], axis=-1, keepdims=True) for c in range(n_slab)) * (1.0 / cw)
    var = sum(jnp.sum(jnp.square(vbuf[c] - mu), axis=-1, keepdims=True) for c in range(n_slab)) * (1.0 / cw)
    rstd = lax.rsqrt(var + EPS)
    y = jnp.zeros((tm, cw), F32)
    for c0 in range(0, cw, CONF_GROUP):
        cs = slice(c0, c0 + CONF_GROUP)
        v = jnp.concatenate([vbuf[c0 // LANES + n] for n in range(CONF_GROUP // LANES)], axis=-1)
        ln = (v - mu) * rstd * lg_ref[:, cs] + lb_ref[:, cs]
        t = _silu(ln) * gbuf[:, cs]
        y = y + jnp.dot(t.astype(BF16), wout_ref[cs, :], preferred_element_type=F32)
    o_ref[0] = x + gate_ref[0] * _rms(y, pog_ref[...])


def _odd_call(x, mod_l, png, pog, win, cw, cb, lg, lb, wout):
    b, s, d = x.shape
    tile = pl.BlockSpec((1, TM, d), lambda i, j: (i, j, 0))
    whole = lambda a: pl.BlockSpec(a.shape, lambda i, j: (0,) * a.ndim)
    modspec = lambda k: pl.BlockSpec((1, 1, d), lambda i, j: (i, 0, k))
    return pl.pallas_call(
        _odd_kernel,
        grid=(b, s // TM),
        in_specs=[tile, modspec(0), modspec(1), modspec(2), whole(png), whole(pog), whole(win),
                  whole(cw), whole(cb), whole(lg), whole(lb), whole(wout)],
        out_specs=tile,
        out_shape=jax.ShapeDtypeStruct((b, s, d), F32),
        scratch_shapes=[pltpu.VMEM((CONF_WIDTH // LANES, TM + CONF_HALO, LANES), F32),
                        pltpu.VMEM((CONF_WIDTH // LANES, TM, LANES), F32),
                        pltpu.VMEM((TM, CONF_WIDTH), F32)],
        compiler_params=pltpu.CompilerParams(dimension_semantics=("arbitrary", "arbitrary"),
                                             vmem_limit_bytes=VMEM_LIMIT),
    )(x, mod_l, mod_l, mod_l, png, pog, win, cw, cb, lg, lb, wout)


def _pack_even_w_in(w):
    d = w.shape[0]
    w = w.astype(BF16)
    a = w[:, :4 * SC_WIDTH]
    o = 4 * SC_WIDTH
    c_q = w[:, o:o + Q_LORA]
    c_kv = w[:, o + Q_LORA:o + Q_LORA + KV_LORA]
    k_r = w[:, o + Q_LORA + KV_LORA:o + Q_LORA + KV_LORA + QK_ROPE]
    b_g = w[:, o + Q_LORA + KV_LORA + QK_ROPE:]
    k_r = jnp.concatenate([jnp.zeros((d, ROPE_LO), w.dtype), k_r,
                           jnp.zeros((d, HEAD_PAD - ROPE_LO - QK_ROPE), w.dtype)], axis=1)
    return jnp.concatenate([a, b_g, c_q, c_kv, k_r], axis=1)


def _pad_heads(w, width):
    k = w.shape[0]
    w = w.astype(BF16).reshape(k, MLA_HEADS, width)
    w = jnp.pad(w, ((0, 0), (0, 0), (0, HEAD_PAD - width)))
    return w.reshape(k, MLA_HEADS * HEAD_PAD)


def kernel(x, c, positions, ada_w, ada_b, pre_norm_g, post_norm_g, even_w_in, even_sc_conv_w, even_sc_conv_b, even_q_norm_g, even_kv_norm_g, even_w_uq, even_w_ukv, even_w_out, odd_w_in, odd_conv_w, odd_conv_b, odd_ln_g, odd_ln_b, odd_w_out):
    b, s, d = x.shape
    assert d == D_MODEL and s % TM == 0 and s % TQ == 0 and TM % TK == 0 and b <= SUBLANES
    assert TQ % CHUNK == 0 and TK % CHUNK == 0 and TQ % TK == 0
    assert CONV_SEG % SUBLANES and CONV_SEG_TAIL % SUBLANES
    assert (TM % (SUBLANES * CONV_SEG)) % (SUBLANES * CONV_SEG_TAIL) == 0

    c_pad = jnp.pad(c, ((0, SUBLANES - b), (0, 0)))
    mod = _mod_call(c_pad, ada_w, ada_b.reshape(DEPTH, 1, 3 * d))

    inv_freq = 1.0 / (ROPE_THETA ** (jnp.arange(0, QK_ROPE, 2, dtype=F32) / QK_ROPE))
    freq_row = jnp.concatenate([jnp.zeros((ROPE_LO,), F32), inv_freq, inv_freq,
                                jnp.zeros((HEAD_PAD - ROPE_LO - QK_ROPE,), F32)]).reshape(1, LANES)
    cos_t, sin_t = _rope_call(positions.reshape(b, s, 1), freq_row)

    row = lambda a: a.reshape(1, -1)
    for layer in range(DEPTH):
        mod_l = mod[layer, :b].reshape(b, 1, 3 * d)
        png, pog = row(pre_norm_g[layer]), row(post_norm_g[layer])
        i = layer // 2
        if layer % 2 == 0:
            w_ukv = even_w_ukv[i].reshape(KV_LORA, MLA_HEADS, QK_NOPE + V_HEAD)
            wuk = _pad_heads(w_ukv[:, :, :QK_NOPE].reshape(KV_LORA, -1), QK_NOPE)
            wuvt = w_ukv[:, :, QK_NOPE:].reshape(KV_LORA, MLA_WIDTH).T.astype(BF16)
            wuq = _pad_heads(even_w_uq[i], QK_NOPE + QK_ROPE)
            ya, bg, q, k, vt = _even_in_call(
                x, mod_l, png, _pack_even_w_in(even_w_in[i]), even_sc_conv_w[i],
                row(even_sc_conv_b[i]), row(even_q_norm_g[i]), row(even_kv_norm_g[i]),
                wuq, wuk, wuvt, cos_t, sin_t)
            x = _attn_out_call(q, k, vt, bg, x, ya, mod_l, pog, even_w_out[i].astype(BF16))
        else:
            x = _odd_call(x, mod_l, png, pog, odd_w_in[i].astype(BF16), odd_conv_w[i],
                          row(odd_conv_b[i]), row(odd_ln_g[i]), row(odd_ln_b[i]),
                          odd_w_out[i].astype(BF16))
    return x
```

```python
import math

import jax
import jax.numpy as jnp
from jax import lax
from jax.experimental import pallas as pl
from jax.experimental.pallas import tpu as pltpu

D_MODEL = 1024
DEPTH = 4
CHUNK = 64
SC_WIDTH = 512
SC_KERNEL = 3
MLA_HEADS = 8
QK_NOPE = 64
QK_ROPE = 32
V_HEAD = 64
Q_LORA = 256
KV_LORA = 128
MLA_WIDTH = MLA_HEADS * V_HEAD
VT_ROWS = V_HEAD + 16
ROPE_THETA = 10000.0
CONF_WIDTH = D_MODEL
CONF_KERNEL = 31
EPS = 1e-6

LANES = 128
HEAD_PAD = LANES
ROPE_LO = QK_NOPE
ROPE_HALF = QK_ROPE // 2
SUBLANES = 8
CONF_HALO = 32

OFF_AB, OFF_AC, OFF_AX, OFF_AG, OFF_BG = 0, 512, 1024, 1536, 2048
OFF_CQ = 2560
OFF_CKV = OFF_CQ + Q_LORA
OFF_KR = OFF_CKV + KV_LORA
EVEN_IN_PAD = OFF_KR + HEAD_PAD

TM = 512
CONF_GROUP = 256
TQ = 256
TK = 256
CONV_SEG_TAIL = 4
CONV_SEG = 12
QK_PARKED = 4
QK_AHEAD = 4
MOD_SPLIT = 4
NEG_BIG = -1e30
VMEM_LIMIT = 56 * 1024 * 1024

F32 = jnp.float32
BF16 = jnp.bfloat16


def _rms(x, g):
    return x * lax.rsqrt(jnp.mean(x * x, axis=-1, keepdims=True) + EPS) * g


def _silu(x):
    return x * jax.nn.sigmoid(x)


def _swap_halves(t, is_t2):
    return jnp.where(is_t2, pltpu.roll(t, ROPE_HALF, 1), pltpu.roll(t, LANES - ROPE_HALF, 1))


def _mod_kernel(c_ref, w_ref, b_ref, o_ref):
    part = jnp.dot(_silu(c_ref[...]).astype(BF16), w_ref[0].astype(BF16), preferred_element_type=F32)

    @pl.when(pl.program_id(1) == 0)
    def _():
        o_ref[0] = b_ref[0] + part

    @pl.when(pl.program_id(1) > 0)
    def _():
        o_ref[0] += part


def _mod_call(c_pad, ada_w, ada_b3):
    d = D_MODEL
    rows = d // MOD_SPLIT
    return pl.pallas_call(
        _mod_kernel,
        grid=(DEPTH, MOD_SPLIT),
        in_specs=[pl.BlockSpec((SUBLANES, rows), lambda l, k: (0, k)),
                  pl.BlockSpec((1, rows, 3 * d), lambda l, k: (l, k, 0)),
                  pl.BlockSpec((1, 1, 3 * d), lambda l, k: (l, 0, 0))],
        out_specs=pl.BlockSpec((1, SUBLANES, 3 * d), lambda l, k: (l, 0, 0)),
        out_shape=jax.ShapeDtypeStruct((DEPTH, SUBLANES, 3 * d), F32),
        compiler_params=pltpu.CompilerParams(dimension_semantics=("arbitrary", "arbitrary")),
    )(c_pad, ada_w, ada_b3)


def _rope_kernel(pos_ref, freq_ref, cos_ref, sin_ref):
    ang = pos_ref[0].astype(F32) * freq_ref[...]
    lane = lax.broadcasted_iota(jnp.int32, ang.shape, 1)
    is_t1 = (lane >= ROPE_LO) & (lane < ROPE_LO + ROPE_HALF)
    is_t2 = (lane >= ROPE_LO + ROPE_HALF) & (lane < ROPE_LO + QK_ROPE)
    c = jnp.cos(ang)
    s = jnp.sin(ang)
    cos_ref[0] = jnp.where(is_t1 | is_t2, c, 1.0)
    sin_ref[0] = jnp.where(is_t1, -s, jnp.where(is_t2, s, 0.0))


def _rope_call(pos3, freq_row):
    b, s, _ = pos3.shape
    ts = 512
    spec = pl.BlockSpec((1, ts, LANES), lambda i, j: (i, j, 0))
    return pl.pallas_call(
        _rope_kernel,
        grid=(b, s // ts),
        in_specs=[
            pl.BlockSpec((1, ts, 1), lambda i, j: (i, j, 0)),
            pl.BlockSpec((1, LANES), lambda i, j: (0, 0)),
        ],
        out_specs=[spec, spec],
        out_shape=[jax.ShapeDtypeStruct((b, s, LANES), F32)] * 2,
        compiler_params=pltpu.CompilerParams(dimension_semantics=("arbitrary", "arbitrary")),
    )(pos3, freq_row)


def _even_in_kernel(x_ref, shift_ref, scale_ref, png_ref, win_ref, scw_ref, scb_ref,
                    qg_ref, kvg_ref, wuq_ref, wuk_ref, wuvt_ref, cos_ref, sin_ref,
                    ya_ref, bg_ref, q_ref, k_ref, vt_ref, ubuf):
    tm = x_ref.shape[1]

    @pl.when(pl.program_id(1) == 0)
    def _():
        ubuf[0:SUBLANES, :] = jnp.zeros((SUBLANES, SC_WIDTH), F32)

    x = x_ref[0]
    h = _rms(x, png_ref[...]) * (1.0 + scale_ref[0]) + shift_ref[0]
    hb = h.astype(BF16)

    def proj(off, width):
        return jnp.dot(hb, win_ref[:, off:off + width], preferred_element_type=F32)

    cos_t = cos_ref[0]
    sin_t = sin_ref[0]
    lane = lax.broadcasted_iota(jnp.int32, (tm, LANES), 1)
    is_t2 = lane >= ROPE_LO + ROPE_HALF

    def rope(t):
        return t * cos_t + _swap_halves(t, is_t2) * sin_t

    z_kv = proj(OFF_CKV, KV_LORA + HEAD_PAD)
    ckv = _rms(z_kv[:, 0:KV_LORA], kvg_ref[...]).astype(BF16)
    vt = lax.dot_general(wuvt_ref[...], ckv, (((1,), (1,)), ((), ())),
                         preferred_element_type=F32).astype(BF16)
    ones_rows = (lax.broadcasted_iota(jnp.int32, (VT_ROWS - V_HEAD, tm), 0) == 0).astype(BF16)
    vt_aug = jnp.concatenate(
        [part for hd in range(MLA_HEADS) for part in (vt[hd * V_HEAD:(hd + 1) * V_HEAD], ones_rows)],
        axis=0)
    for j in range(tm // TK):
        vt_ref[0, j] = vt_aug[:, j * TK:(j + 1) * TK]
    kn = jnp.dot(ckv, wuk_ref[...], preferred_element_type=F32)
    kr = rope(z_kv[:, KV_LORA:KV_LORA + HEAD_PAD])
    for hd in range(MLA_HEADS):
        sl = slice(hd * HEAD_PAD, (hd + 1) * HEAD_PAD)
        k_ref[0, :, sl] = (kn[:, sl] + kr).astype(BF16)

    cq = _rms(proj(OFF_CQ, Q_LORA), qg_ref[...]).astype(BF16)
    qf = jnp.dot(cq, wuq_ref[...], preferred_element_type=F32)
    q_scale = math.log2(math.e) / math.sqrt(QK_NOPE + QK_ROPE)
    for hd in range(MLA_HEADS):
        sl = slice(hd * HEAD_PAD, (hd + 1) * HEAD_PAD)
        q_ref[0, :, sl] = (rope(qf[:, sl]) * q_scale).astype(BF16)

    u = proj(OFF_AC, SC_WIDTH) * proj(OFF_AX, SC_WIDTH)
    ubuf[SUBLANES:SUBLANES + tm, :] = u
    conv = (scw_ref[2:3, :] * u
            + scw_ref[1:2, :] * ubuf[SUBLANES - 1:SUBLANES - 1 + tm, :]
            + scw_ref[0:1, :] * ubuf[SUBLANES - 2:SUBLANES - 2 + tm, :]
            + scb_ref[...])
    ubuf[0:SUBLANES, :] = ubuf[tm:tm + SUBLANES, :]
    y_a = proj(OFF_AB, SC_WIDTH) * conv * _silu(proj(OFF_AG, SC_WIDTH))
    ya_ref[0] = y_a.astype(BF16)
    bg_ref[0] = _silu(proj(OFF_BG, MLA_WIDTH)).astype(BF16)


def _even_in_call(x, mod_l, png, win, scw, scb, qg, kvg, wuq, wuk, wuvt, cos_t, sin_t):
    b, s, d = x.shape
    hp = MLA_HEADS * HEAD_PAD
    tile = lambda w: pl.BlockSpec((1, TM, w), lambda i, j: (i, j, 0))
    whole = lambda a: pl.BlockSpec(a.shape, lambda i, j: (0,) * a.ndim)
    modspec = lambda k: pl.BlockSpec((1, 1, d), lambda i, j: (i, 0, k))
    return pl.pallas_call(
        _even_in_kernel,
        grid=(b, s // TM),
        in_specs=[tile(d), modspec(0), modspec(1), whole(png), whole(win), whole(scw), whole(scb),
                  whole(qg), whole(kvg), whole(wuq), whole(wuk), whole(wuvt), tile(LANES), tile(LANES)],
        out_specs=[tile(SC_WIDTH), tile(MLA_WIDTH), tile(hp), tile(hp),
                   pl.BlockSpec((1, TM // TK, MLA_HEADS * VT_ROWS, TK), lambda i, j: (i, j, 0, 0))],
        out_shape=[jax.ShapeDtypeStruct((b, s, SC_WIDTH), BF16),
                   jax.ShapeDtypeStruct((b, s, MLA_WIDTH), BF16),
                   jax.ShapeDtypeStruct((b, s, hp), BF16),
                   jax.ShapeDtypeStruct((b, s, hp), BF16),
                   jax.ShapeDtypeStruct((b, s // TK, MLA_HEADS * VT_ROWS, TK), BF16)],
        scratch_shapes=[pltpu.VMEM((TM + SUBLANES, SC_WIDTH), F32)],
        compiler_params=pltpu.CompilerParams(dimension_semantics=("arbitrary", "arbitrary"),
                                             vmem_limit_bytes=VMEM_LIMIT),
    )(x, mod_l, mod_l, png, win, scw, scb, qg, kvg, wuq, wuk, wuvt, cos_t, sin_t)


def _attn_out_kernel(q_ref, k_ref, vt_ref, bg_ref, x_ref, ya_ref, gate_ref, pog_ref, wout_ref,
                     o_ref, m_s, acc_s, yb_s, s_pre, ya_s):
    qi = pl.program_id(1)
    m_s[...] = jnp.full(m_s.shape, NEG_BIG, F32)
    acc_s[...] = jnp.zeros(acc_s.shape, F32)

    key_chunk = lax.broadcasted_iota(jnp.int32, (TK, TQ), 0) // CHUNK
    qry_chunk = lax.broadcasted_iota(jnp.int32, (TK, TQ), 1) // CHUNK

    def scores(kb, hd):
        rows = pl.ds(pl.multiple_of(kb * TK, TK), TK)
        sl = slice(hd * HEAD_PAD, (hd + 1) * HEAD_PAD)
        return lax.dot_general(k_ref[0, rows, sl], q_ref[0, :, sl], (((1,), (1,)), ((), ())),
                               preferred_element_type=F32)

    def block(kb, diag=None, last=False):
        masked = diag is not None

        def update(hd, s):
            if masked:
                s = jnp.where(key_chunk + diag * (TK // CHUNK) <= qry_chunk, s, NEG_BIG)
            m_prev = m_s[hd]
            m_new = jnp.maximum(m_prev, jnp.max(s, axis=0, keepdims=True))
            alpha = jnp.exp2(m_prev - m_new)
            p = jnp.exp2(s - m_new)
            pv = jnp.dot(vt_ref[0, kb, hd * VT_ROWS:(hd + 1) * VT_ROWS, :], p.astype(BF16),
                         preferred_element_type=F32)
            acc_s[hd] = alpha * acc_s[hd] + pv
            m_s[hd] = m_new

        pending = {hd: s_pre[hd] for hd in range(QK_PARKED)}
        for hd in range(QK_PARKED, QK_AHEAD):
            pending[hd] = scores(kb, hd)
        for hd in range(MLA_HEADS):
            ahead = hd + QK_AHEAD
            if ahead < MLA_HEADS:
                pending[ahead] = scores(kb, ahead)
            elif not last and ahead - MLA_HEADS < QK_PARKED:
                s_pre[ahead - MLA_HEADS] = scores(kb + 1, ahead - MLA_HEADS)
            elif last:
                width = D_MODEL // QK_AHEAD
                cols = slice((ahead - MLA_HEADS) * width, (ahead - MLA_HEADS + 1) * width)
                ya_s[:, cols] = jnp.dot(ya_ref[0], wout_ref[0:SC_WIDTH, cols],
                                        preferred_element_type=F32)
            update(hd, pending.pop(hd))

    def pair_body(kk, carry):
        block(2 * kk)
        block(2 * kk + 1)
        return carry

    def single_body(kb, carry):
        block(kb)
        return carry

    span = TQ // TK
    visible = qi * span
    for hd in range(QK_PARKED):
        s_pre[hd] = scores(0, hd)
    lax.fori_loop(0, lax.shift_right_logical(visible, 1), pair_body, 0)
    lax.fori_loop(visible - (visible & 1), visible, single_body, 0)
    for d in range(span):
        block(qi * span + d, diag=d, last=(d == span - 1))

    for pair in range(MLA_HEADS // 2):
        both = jnp.concatenate(
            [acc_s[hd, 0:V_HEAD, :] * (1.0 / acc_s[hd, V_HEAD:V_HEAD + 1, :])
             for hd in (2 * pair, 2 * pair + 1)], axis=0)
        cols = slice(pair * LANES, (pair + 1) * LANES)
        yb_s[:, cols] = (both.T * bg_ref[0, :, cols].astype(F32)).astype(BF16)
    y = ya_s[...] + jnp.dot(yb_s[...], wout_ref[SC_WIDTH:SC_WIDTH + MLA_WIDTH, :],
                            preferred_element_type=F32)
    o_ref[0] = x_ref[0] + gate_ref[0] * _rms(y, pog_ref[...])


def _attn_out_call(q, k, vt, bg, x, ya, mod_l, pog, wout):
    b, s, d = x.shape
    hp = MLA_HEADS * HEAD_PAD
    tile = lambda w: pl.BlockSpec((1, TQ, w), lambda i, j: (i, j, 0))
    whole = lambda a: pl.BlockSpec(a.shape, lambda i, j: (0,) * a.ndim)
    return pl.pallas_call(
        _attn_out_kernel,
        grid=(b, s // TQ),
        in_specs=[tile(hp),
                  pl.BlockSpec((1, s, hp), lambda i, j: (i, 0, 0)),
                  pl.BlockSpec((1, s // TK, MLA_HEADS * VT_ROWS, TK), lambda i, j: (i, 0, 0, 0)),
                  tile(MLA_WIDTH), tile(d), tile(SC_WIDTH),
                  pl.BlockSpec((1, 1, d), lambda i, j: (i, 0, 2)), whole(pog), whole(wout)],
        out_specs=tile(d),
        out_shape=jax.ShapeDtypeStruct((b, s, d), F32),
        scratch_shapes=[pltpu.VMEM((MLA_HEADS, 1, TQ), F32),
                        pltpu.VMEM((MLA_HEADS, VT_ROWS, TQ), F32),
                        pltpu.VMEM((TQ, MLA_WIDTH), BF16),
                        pltpu.VMEM((QK_PARKED, TK, TQ), F32),
                        pltpu.VMEM((TQ, D_MODEL), F32)],
        compiler_params=pltpu.CompilerParams(dimension_semantics=("arbitrary", "arbitrary"),
                                             vmem_limit_bytes=VMEM_LIMIT),
    )(q, k, vt, bg, x, ya, mod_l, pog, wout)


def _odd_kernel(x_ref, shift_ref, scale_ref, gate_ref, png_ref, pog_ref, win_ref, cw_ref, cb_ref,
                lg_ref, lb_ref, wout_ref, o_ref, ubuf, vbuf, gbuf):
    tm = x_ref.shape[1]
    cw = CONF_WIDTH

    @pl.when(pl.program_id(1) == 0)
    def _():
        ubuf[:, 0:CONF_HALO, :] = jnp.zeros((cw // LANES, CONF_HALO, LANES), F32)

    x = x_ref[0]
    hb = (_rms(x, png_ref[...]) * (1.0 + scale_ref[0]) + shift_ref[0]).astype(BF16)

    def proj(off):
        return jnp.dot(hb, win_ref[:, off:off + CONF_GROUP], preferred_element_type=F32)

    groups, g0 = [], 0
    while g0 < tm:
        seg = CONV_SEG if tm - g0 >= SUBLANES * CONV_SEG else CONV_SEG_TAIL
        groups.append((g0, seg))
        g0 += SUBLANES * seg

    for c0 in range(0, cw, CONF_GROUP):
        u = proj(c0) * jax.nn.sigmoid(proj(cw + c0))
        gbuf[:, c0:c0 + CONF_GROUP] = _silu(proj(2 * cw + c0))
        for n in range(CONF_GROUP // LANES):
            slab = c0 // LANES + n
            cs = slice(slab * LANES, (slab + 1) * LANES)
            ubuf[slab, CONF_HALO:CONF_HALO + tm, :] = u[:, n * LANES:(n + 1) * LANES]
            for g0, seg in groups:
                taps = {m: ubuf[slab, pl.ds(CONF_HALO + g0 + m, SUBLANES, stride=seg), :]
                        for m in range(1 - CONF_KERNEL, seg)}
                acc = [jnp.broadcast_to(cb_ref[:, cs], (SUBLANES, LANES)) for _ in range(seg)]
                for j in range(CONF_KERNEL):
                    w_j = jnp.broadcast_to(cw_ref[CONF_KERNEL - 1 - j:CONF_KERNEL - j, cs], (SUBLANES, LANES))
                    for i in range(seg):
                        acc[i] = acc[i] + w_j * taps[i - j]
                for i in range(seg):
                    vbuf[slab, pl.ds(g0 + i, SUBLANES, stride=seg), :] = acc[i]
    ubuf[:, 0:CONF_HALO, :] = ubuf[:, tm:tm + CONF_HALO, :]

    n_slab = cw // LANES
    mu = sum(jnp.sum(vbuf[c], axis=-1, keepdims=True) for c in range(n_slab)) * (1.0 / cw)
    var = sum(jnp.sum(jnp.square(vbuf[c] - mu), axis=-1, keepdims=True) for c in range(n_slab)) * (1.0 / cw)
    rstd = lax.rsqrt(var + EPS)
    y = jnp.zeros((tm, cw), F32)
    for c0 in range(0, cw, CONF_GROUP):
        cs = slice(c0, c0 + CONF_GROUP)
        v = jnp.concatenate([vbuf[c0 // LANES + n] for n in range(CONF_GROUP // LANES)], axis=-1)
        ln = (v - mu) * rstd * lg_ref[:, cs] + lb_ref[:, cs]
        t = _silu(ln) * gbuf[:, cs]
        y = y + jnp.dot(t.astype(BF16), wout_ref[cs, :], preferred_element_type=F32)
    o_ref[0] = x + gate_ref[0] * _rms(y, pog_ref[...])


def _odd_call(x, mod_l, png, pog, win, cw, cb, lg, lb, wout):
    b, s, d = x.shape
    tile = pl.BlockSpec((1, TM, d), lambda i, j: (i, j, 0))
    whole = lambda a: pl.BlockSpec(a.shape, lambda i, j: (0,) * a.ndim)
    modspec = lambda k: pl.BlockSpec((1, 1, d), lambda i, j: (i, 0, k))
    return pl.pallas_call(
        _odd_kernel,
        grid=(b, s // TM),
        in_specs=[tile, modspec(0), modspec(1), modspec(2), whole(png), whole(pog), whole(win),
                  whole(cw), whole(cb), whole(lg), whole(lb), whole(wout)],
        out_specs=tile,
        out_shape=jax.ShapeDtypeStruct((b, s, d), F32),
        scratch_shapes=[pltpu.VMEM((CONF_WIDTH // LANES, TM + CONF_HALO, LANES), F32),
                        pltpu.VMEM((CONF_WIDTH // LANES, TM, LANES), F32),
                        pltpu.VMEM((TM, CONF_WIDTH), F32)],
        compiler_params=pltpu.CompilerParams(dimension_semantics=("arbitrary", "arbitrary"),
                                             vmem_limit_bytes=VMEM_LIMIT),
    )(x, mod_l, mod_l, mod_l, png, pog, win, cw, cb, lg, lb, wout)


def _pack_even_w_in(w):
    d = w.shape[0]
    w = w.astype(BF16)
    a = w[:, :4 * SC_WIDTH]
    o = 4 * SC_WIDTH
    c_q = w[:, o:o + Q_LORA]
    c_kv = w[:, o + Q_LORA:o + Q_LORA + KV_LORA]
    k_r = w[:, o + Q_LORA + KV_LORA:o + Q_LORA + KV_LORA + QK_ROPE]
    b_g = w[:, o + Q_LORA + KV_LORA + QK_ROPE:]
    k_r = jnp.concatenate([jnp.zeros((d, ROPE_LO), w.dtype), k_r,
                           jnp.zeros((d, HEAD_PAD - ROPE_LO - QK_ROPE), w.dtype)], axis=1)
    return jnp.concatenate([a, b_g, c_q, c_kv, k_r], axis=1)


def _pad_heads(w, width):
    k = w.shape[0]
    w = w.astype(BF16).reshape(k, MLA_HEADS, width)
    w = jnp.pad(w, ((0, 0), (0, 0), (0, HEAD_PAD - width)))
    return w.reshape(k, MLA_HEADS * HEAD_PAD)


def kernel(x, c, positions, ada_w, ada_b, pre_norm_g, post_norm_g, even_w_in, even_sc_conv_w, even_sc_conv_b, even_q_norm_g, even_kv_norm_g, even_w_uq, even_w_ukv, even_w_out, odd_w_in, odd_conv_w, odd_conv_b, odd_ln_g, odd_ln_b, odd_w_out):
    b, s, d = x.shape
    assert d == D_MODEL and s % TM == 0 and s % TQ == 0 and TM % TK == 0 and b <= SUBLANES
    assert TQ % CHUNK == 0 and TK % CHUNK == 0 and TQ % TK == 0
    assert CONV_SEG % SUBLANES and CONV_SEG_TAIL % SUBLANES
    assert (TM % (SUBLANES * CONV_SEG)) % (SUBLANES * CONV_SEG_TAIL) == 0

    c_pad = jnp.pad(c, ((0, SUBLANES - b), (0, 0)))
    mod = _mod_call(c_pad, ada_w, ada_b.reshape(DEPTH, 1, 3 * d))

    inv_freq = 1.0 / (ROPE_THETA ** (jnp.arange(0, QK_ROPE, 2, dtype=F32) / QK_ROPE))
    freq_row = jnp.concatenate([jnp.zeros((ROPE_LO,), F32), inv_freq, inv_freq,
                                jnp.zeros((HEAD_PAD - ROPE_LO - QK_ROPE,), F32)]).reshape(1, LANES)
    cos_t, sin_t = _rope_call(positions.reshape(b, s, 1), freq_row)

    row = lambda a: a.reshape(1, -1)
    for layer in range(DEPTH):
        mod_l = mod[layer, :b].reshape(b, 1, 3 * d)
        png, pog = row(pre_norm_g[layer]), row(post_norm_g[layer])
        i = layer // 2
        if layer % 2 == 0:
            w_ukv = even_w_ukv[i].reshape(KV_LORA, MLA_HEADS, QK_NOPE + V_HEAD)
            wuk = _pad_heads(w_ukv[:, :, :QK_NOPE].reshape(KV_LORA, -1), QK_NOPE)
            wuvt = w_ukv[:, :, QK_NOPE:].reshape(KV_LORA, MLA_WIDTH).T.astype(BF16)
            wuq = _pad_heads(even_w_uq[i], QK_NOPE + QK_ROPE)
            ya, bg, q, k, vt = _even_in_call(
                x, mod_l, png, _pack_even_w_in(even_w_in[i]), even_sc_conv_w[i],
                row(even_sc_conv_b[i]), row(even_q_norm_g[i]), row(even_kv_norm_g[i]),
                wuq, wuk, wuvt, cos_t, sin_t)
            x = _attn_out_call(q, k, vt, bg, x, ya, mod_l, pog, even_w_out[i].astype(BF16))
        else:
            x = _odd_call(x, mod_l, png, pog, odd_w_in[i].astype(BF16), odd_conv_w[i],
                          row(odd_conv_b[i]), row(odd_ln_g[i]), row(odd_ln_b[i]),
                          odd_w_out[i].astype(BF16))
    return x
```

```python
import math

import jax
import jax.numpy as jnp
from jax import lax
from jax.experimental import pallas as pl
from jax.experimental.pallas import tpu as pltpu

D_MODEL = 1024
DEPTH = 4
CHUNK = 64
SC_WIDTH = 512
SC_KERNEL = 3
MLA_HEADS = 8
QK_NOPE = 64
QK_ROPE = 32
V_HEAD = 64
Q_LORA = 256
KV_LORA = 128
MLA_WIDTH = MLA_HEADS * V_HEAD
VT_ROWS = V_HEAD + 16
ROPE_THETA = 10000.0
CONF_WIDTH = D_MODEL
CONF_KERNEL = 31
EPS = 1e-6

LANES = 128
HEAD_PAD = LANES
ROPE_LO = QK_NOPE
ROPE_HALF = QK_ROPE // 2
ROPE_PACK = LANES // QK_ROPE
SUBLANES = 8
CONF_HALO = 32

OFF_AB, OFF_AC, OFF_AX, OFF_AG, OFF_BG = 0, 512, 1024, 1536, 2048
OFF_CQ = 2560
OFF_CKV = OFF_CQ + Q_LORA
OFF_KR = OFF_CKV + KV_LORA
EVEN_IN_PAD = OFF_KR + HEAD_PAD

TM = 512
CONF_GROUP = 256
TQ = 256
TK = 256
CONV_SEG_TAIL = 4
CONV_SEG = 12
QK_PARKED = 4
QK_AHEAD = 4
MOD_SPLIT = 4
NEG_BIG = -1e30
VMEM_LIMIT = 56 * 1024 * 1024

F32 = jnp.float32
BF16 = jnp.bfloat16


def _rms(x, g):
    return x * lax.rsqrt(jnp.mean(x * x, axis=-1, keepdims=True) + EPS) * g


def _silu(x):
    return x * jax.nn.sigmoid(x)


def _swap_halves(t, is_t2):
    return jnp.where(is_t2, pltpu.roll(t, ROPE_HALF, 1), pltpu.roll(t, LANES - ROPE_HALF, 1))


def _mod_kernel(c_ref, *refs):
    w_refs, b_ref, o_ref = refs[:MOD_SPLIT], refs[MOD_SPLIT], refs[MOD_SPLIT + 1]
    a = _silu(c_ref[...]).astype(BF16)
    rows = D_MODEL // MOD_SPLIT
    acc = b_ref[0]
    for n, w_ref in enumerate(w_refs):
        acc = acc + jnp.dot(a[:, n * rows:(n + 1) * rows], w_ref[0].astype(BF16),
                            preferred_element_type=F32)
    o_ref[0] = acc


def _mod_call(c_pad, ada_w, ada_b3):
    d = D_MODEL
    rows = d // MOD_SPLIT
    band = lambda n: pl.BlockSpec((1, rows, d), lambda l, j: (l, n, j))
    return pl.pallas_call(
        _mod_kernel,
        grid=(DEPTH, 3),
        in_specs=[pl.BlockSpec((SUBLANES, d), lambda l, j: (0, 0))]
                 + [band(n) for n in range(MOD_SPLIT)]
                 + [pl.BlockSpec((1, 1, d), lambda l, j: (l, 0, j))],
        out_specs=pl.BlockSpec((1, SUBLANES, d), lambda l, j: (l, 0, j)),
        out_shape=jax.ShapeDtypeStruct((DEPTH, SUBLANES, 3 * d), F32),
        compiler_params=pltpu.CompilerParams(dimension_semantics=("arbitrary", "arbitrary")),
    )(c_pad, *([ada_w] * MOD_SPLIT), ada_b3)


def _rope_kernel(pos_ref, freq_ref, cos_ref, sin_ref):
    ang = pos_ref[0].astype(F32) * freq_ref[...]
    rows = ang.shape[0]
    lane = lax.broadcasted_iota(jnp.int32, ang.shape, 1)
    is_t1 = (lane >= ROPE_LO) & (lane < ROPE_LO + ROPE_HALF)
    is_t2 = (lane >= ROPE_LO + ROPE_HALF) & (lane < ROPE_LO + QK_ROPE)
    c = jnp.cos(ang)
    s = jnp.sin(ang)
    for q in range(ROPE_PACK):
        shift = (ROPE_LO - QK_ROPE * q) % LANES
        cq = c if shift == 0 else pltpu.roll(c, shift, 1)
        sq = s if shift == 0 else pltpu.roll(s, shift, 1)
        out_rows = pl.ds(q, rows, stride=ROPE_PACK)
        cos_ref[0, out_rows, :] = jnp.where(is_t1 | is_t2, cq, 1.0)
        sin_ref[0, out_rows, :] = jnp.where(is_t1, -sq, jnp.where(is_t2, sq, 0.0))


def _rope_call(pos_packed, freq_row):
    b, rows, _ = pos_packed.shape
    s = rows * ROPE_PACK
    ts = 512
    spec = pl.BlockSpec((1, ts, LANES), lambda i, j: (i, j, 0))
    return pl.pallas_call(
        _rope_kernel,
        grid=(b, s // ts),
        in_specs=[
            pl.BlockSpec((1, ts // ROPE_PACK, LANES), lambda i, j: (i, j, 0)),
            pl.BlockSpec((1, LANES), lambda i, j: (0, 0)),
        ],
        out_specs=[spec, spec],
        out_shape=[jax.ShapeDtypeStruct((b, s, LANES), F32)] * 2,
        compiler_params=pltpu.CompilerParams(dimension_semantics=("arbitrary", "arbitrary")),
    )(pos_packed, freq_row)


def _even_in_kernel(x_ref, shift_ref, scale_ref, png_ref, win_ref, scw_ref, scb_ref,
                    qg_ref, kvg_ref, wuq_ref, wuk_ref, wuvt_ref, cos_ref, sin_ref,
                    ya_ref, bg_ref, q_ref, k_ref, vt_ref, ubuf):
    tm = x_ref.shape[1]

    @pl.when(pl.program_id(1) == 0)
    def _():
        ubuf[0:SUBLANES, :] = jnp.zeros((SUBLANES, SC_WIDTH), F32)

    x = x_ref[0]
    h = _rms(x, png_ref[...]) * (1.0 + scale_ref[0]) + shift_ref[0]
    hb = h.astype(BF16)

    def proj(off, width):
        return jnp.dot(hb, win_ref[:, off:off + width], preferred_element_type=F32)

    cos_t = cos_ref[0]
    sin_t = sin_ref[0]
    lane = lax.broadcasted_iota(jnp.int32, (tm, LANES), 1)
    is_t2 = lane >= ROPE_LO + ROPE_HALF

    def rope(t):
        return t * cos_t + _swap_halves(t, is_t2) * sin_t

    z_kv = proj(OFF_CKV, KV_LORA + HEAD_PAD)
    ckv = _rms(z_kv[:, 0:KV_LORA], kvg_ref[...]).astype(BF16)
    vt = lax.dot_general(wuvt_ref[...], ckv, (((1,), (1,)), ((), ())),
                         preferred_element_type=F32).astype(BF16)
    ones_rows = (lax.broadcasted_iota(jnp.int32, (VT_ROWS - V_HEAD, tm), 0) == 0).astype(BF16)
    vt_aug = jnp.concatenate(
        [part for hd in range(MLA_HEADS) for part in (vt[hd * V_HEAD:(hd + 1) * V_HEAD], ones_rows)],
        axis=0)
    for j in range(tm // TK):
        vt_ref[0, j] = vt_aug[:, j * TK:(j + 1) * TK]
    kn = jnp.dot(ckv, wuk_ref[...], preferred_element_type=F32)
    kr = rope(z_kv[:, KV_LORA:KV_LORA + HEAD_PAD])
    for hd in range(MLA_HEADS):
        sl = slice(hd * HEAD_PAD, (hd + 1) * HEAD_PAD)
        k_ref[0, :, sl] = (kn[:, sl] + kr).astype(BF16)

    cq = _rms(proj(OFF_CQ, Q_LORA), qg_ref[...]).astype(BF16)
    qf = jnp.dot(cq, wuq_ref[...], preferred_element_type=F32)
    q_scale = math.log2(math.e) / math.sqrt(QK_NOPE + QK_ROPE)
    for hd in range(MLA_HEADS):
        sl = slice(hd * HEAD_PAD, (hd + 1) * HEAD_PAD)
        q_ref[0, :, sl] = (rope(qf[:, sl]) * q_scale).astype(BF16)

    u = proj(OFF_AC, SC_WIDTH) * proj(OFF_AX, SC_WIDTH)
    ubuf[SUBLANES:SUBLANES + tm, :] = u
    conv = (scw_ref[2:3, :] * u
            + scw_ref[1:2, :] * ubuf[SUBLANES - 1:SUBLANES - 1 + tm, :]
            + scw_ref[0:1, :] * ubuf[SUBLANES - 2:SUBLANES - 2 + tm, :]
            + scb_ref[...])
    ubuf[0:SUBLANES, :] = ubuf[tm:tm + SUBLANES, :]
    y_a = proj(OFF_AB, SC_WIDTH) * conv * _silu(proj(OFF_AG, SC_WIDTH))
    ya_ref[0] = y_a.astype(BF16)
    bg_ref[0] = _silu(proj(OFF_BG, MLA_WIDTH)).astype(BF16)


def _even_in_call(x, mod_l, png, win, scw, scb, qg, kvg, wuq, wuk, wuvt, cos_t, sin_t):
    b, s, d = x.shape
    hp = MLA_HEADS * HEAD_PAD
    tile = lambda w: pl.BlockSpec((1, TM, w), lambda i, j: (i, j, 0))
    whole = lambda a: pl.BlockSpec(a.shape, lambda i, j: (0,) * a.ndim)
    modspec = lambda k: pl.BlockSpec((1, 1, d), lambda i, j: (i, 0, k))
    return pl.pallas_call(
        _even_in_kernel,
        grid=(b, s // TM),
        in_specs=[tile(d), modspec(0), modspec(1), whole(png), whole(win), whole(scw), whole(scb),
                  whole(qg), whole(kvg), whole(wuq), whole(wuk), whole(wuvt), tile(LANES), tile(LANES)],
        out_specs=[tile(SC_WIDTH), tile(MLA_WIDTH), tile(hp), tile(hp),
                   pl.BlockSpec((1, TM // TK, MLA_HEADS * VT_ROWS, TK), lambda i, j: (i, j, 0, 0))],
        out_shape=[jax.ShapeDtypeStruct((b, s, SC_WIDTH), BF16),
                   jax.ShapeDtypeStruct((b, s, MLA_WIDTH), BF16),
                   jax.ShapeDtypeStruct((b, s, hp), BF16),
                   jax.ShapeDtypeStruct((b, s, hp), BF16),
                   jax.ShapeDtypeStruct((b, s // TK, MLA_HEADS * VT_ROWS, TK), BF16)],
        scratch_shapes=[pltpu.VMEM((TM + SUBLANES, SC_WIDTH), F32)],
        compiler_params=pltpu.CompilerParams(dimension_semantics=("arbitrary", "arbitrary"),
                                             vmem_limit_bytes=VMEM_LIMIT),
    )(x, mod_l, mod_l, png, win, scw, scb, qg, kvg, wuq, wuk, wuvt, cos_t, sin_t)


def _attn_out_kernel(q_ref, k_ref, vt_ref, bg_ref, x_ref, ya_ref, gate_ref, pog_ref, wout_ref,
                     o_ref, m_s, acc_s, yb_s, s_pre, ya_s):
    qi = pl.program_id(1)
    m_s[...] = jnp.full(m_s.shape, NEG_BIG, F32)
    acc_s[...] = jnp.zeros(acc_s.shape, F32)

    key_chunk = lax.broadcasted_iota(jnp.int32, (TK, TQ), 0) // CHUNK
    qry_chunk = lax.broadcasted_iota(jnp.int32, (TK, TQ), 1) // CHUNK

    def scores(kb, hd):
        rows = pl.ds(pl.multiple_of(kb * TK, TK), TK)
        sl = slice(hd * HEAD_PAD, (hd + 1) * HEAD_PAD)
        return lax.dot_general(k_ref[0, rows, sl], q_ref[0, :, sl], (((1,), (1,)), ((), ())),
                               preferred_element_type=F32)

    def block(kb, diag=None, last=False):
        masked = diag is not None

        def update(hd, s):
            if masked:
                s = jnp.where(key_chunk + diag * (TK // CHUNK) <= qry_chunk, s, NEG_BIG)
            m_prev = m_s[hd]
            m_new = jnp.maximum(m_prev, jnp.max(s, axis=0, keepdims=True))
            alpha = jnp.exp2(m_prev - m_new)
            p = jnp.exp2(s - m_new)
            pv = jnp.dot(vt_ref[0, kb, hd * VT_ROWS:(hd + 1) * VT_ROWS, :], p.astype(BF16),
                         preferred_element_type=F32)
            acc_s[hd] = alpha * acc_s[hd] + pv
            m_s[hd] = m_new

        pending = {hd: s_pre[hd] for hd in range(QK_PARKED)}
        for hd in range(QK_PARKED, QK_AHEAD):
            pending[hd] = scores(kb, hd)
        for hd in range(MLA_HEADS):
            ahead = hd + QK_AHEAD
            if ahead < MLA_HEADS:
                pending[ahead] = scores(kb, ahead)
            elif not last and ahead - MLA_HEADS < QK_PARKED:
                s_pre[ahead - MLA_HEADS] = scores(kb + 1, ahead - MLA_HEADS)
            elif last:
                width = D_MODEL // QK_AHEAD
                cols = slice((ahead - MLA_HEADS) * width, (ahead - MLA_HEADS + 1) * width)
                ya_s[:, cols] = jnp.dot(ya_ref[0], wout_ref[0:SC_WIDTH, cols],
                                        preferred_element_type=F32)
            update(hd, pending.pop(hd))

    def pair_body(kk, carry):
        block(2 * kk)
        block(2 * kk + 1)
        return carry

    def single_body(kb, carry):
        block(kb)
        return carry

    span = TQ // TK
    visible = qi * span
    for hd in range(QK_PARKED):
        s_pre[hd] = scores(0, hd)
    lax.fori_loop(0, lax.shift_right_logical(visible, 1), pair_body, 0)
    lax.fori_loop(visible - (visible & 1), visible, single_body, 0)
    for d in range(span):
        block(qi * span + d, diag=d, last=(d == span - 1))

    for pair in range(MLA_HEADS // 2):
        both = jnp.concatenate(
            [acc_s[hd, 0:V_HEAD, :] * (1.0 / acc_s[hd, V_HEAD:V_HEAD + 1, :])
             for hd in (2 * pair, 2 * pair + 1)], axis=0)
        cols = slice(pair * LANES, (pair + 1) * LANES)
        yb_s[:, cols] = (both.T * bg_ref[0, :, cols].astype(F32)).astype(BF16)
    y = ya_s[...] + jnp.dot(yb_s[...], wout_ref[SC_WIDTH:SC_WIDTH + MLA_WIDTH, :],
                            preferred_element_type=F32)
    o_ref[0] = x_ref[0] + gate_ref[0] * _rms(y, pog_ref[...])


def _attn_out_call(q, k, vt, bg, x, ya, mod_l, pog, wout):
    b, s, d = x.shape
    hp = MLA_HEADS * HEAD_PAD
    tile = lambda w: pl.BlockSpec((1, TQ, w), lambda i, j: (i, j, 0))
    whole = lambda a: pl.BlockSpec(a.shape, lambda i, j: (0,) * a.ndim)
    return pl.pallas_call(
        _attn_out_kernel,
        grid=(b, s // TQ),
        in_specs=[tile(hp),
                  pl.BlockSpec((1, s, hp), lambda i, j: (i, 0, 0)),
                  pl.BlockSpec((1, s // TK, MLA_HEADS * VT_ROWS, TK), lambda i, j: (i, 0, 0, 0)),
                  tile(MLA_WIDTH), tile(d), tile(SC_WIDTH),
                  pl.BlockSpec((1, 1, d), lambda i, j: (i, 0, 2)), whole(pog), whole(wout)],
        out_specs=tile(d),
        out_shape=jax.ShapeDtypeStruct((b, s, d), F32),
        scratch_shapes=[pltpu.VMEM((MLA_HEADS, 1, TQ), F32),
                        pltpu.VMEM((MLA_HEADS, VT_ROWS, TQ), F32),
                        pltpu.VMEM((TQ, MLA_WIDTH), BF16),
                        pltpu.VMEM((QK_PARKED, TK, TQ), F32),
                        pltpu.VMEM((TQ, D_MODEL), F32)],
        compiler_params=pltpu.CompilerParams(dimension_semantics=("arbitrary", "arbitrary"),
                                             vmem_limit_bytes=VMEM_LIMIT),
    )(q, k, vt, bg, x, ya, mod_l, pog, wout)


def _odd_kernel(x_ref, shift_ref, scale_ref, gate_ref, png_ref, pog_ref, win_ref, cw_ref, cb_ref,
                lg_ref, lb_ref, wout_ref, o_ref, ubuf, vbuf, gbuf):
    tm = x_ref.shape[1]
    cw = CONF_WIDTH

    @pl.when(pl.program_id(1) == 0)
    def _():
        ubuf[:, 0:CONF_HALO, :] = jnp.zeros((cw // LANES, CONF_HALO, LANES), F32)

    x = x_ref[0]
    hb = (_rms(x, png_ref[...]) * (1.0 + scale_ref[0]) + shift_ref[0]).astype(BF16)

    def proj(off):
        return jnp.dot(hb, win_ref[:, off:off + CONF_GROUP], preferred_element_type=F32)

    groups, g0 = [], 0
    while g0 < tm:
        seg = CONV_SEG if tm - g0 >= SUBLANES * CONV_SEG else CONV_SEG_TAIL
        groups.append((g0, seg))
        g0 += SUBLANES * seg

    for c0 in range(0, cw, CONF_GROUP):
        u = proj(c0) * jax.nn.sigmoid(proj(cw + c0))
        gbuf[:, c0:c0 + CONF_GROUP] = _silu(proj(2 * cw + c0))
        for n in range(CONF_GROUP // LANES):
            slab = c0 // LANES + n
            cs = slice(slab * LANES, (slab + 1) * LANES)
            ubuf[slab, CONF_HALO:CONF_HALO + tm, :] = u[:, n * LANES:(n + 1) * LANES]
            for g0, seg in groups:
                taps = {m: ubuf[slab, pl.ds(CONF_HALO + g0 + m, SUBLANES, stride=seg), :]
                        for m in range(1 - CONF_KERNEL, seg)}
                acc = [jnp.broadcast_to(cb_ref[:, cs], (SUBLANES, LANES)) for _ in range(seg)]
                for j in range(CONF_KERNEL):
                    w_j = jnp.broadcast_to(cw_ref[CONF_KERNEL - 1 - j:CONF_KERNEL - j, cs], (SUBLANES, LANES))
                    for i in range(seg):
                        acc[i] = acc[i] + w_j * taps[i - j]
                for i in range(seg):
                    vbuf[slab, pl.ds(g0 + i, SUBLANES, stride=seg), :] = acc[i]
    ubuf[:, 0:CONF_HALO, :] = ubuf[:, tm:tm + CONF_HALO, :]

    n_slab = cw // LANES
    mu = sum(jnp.sum(vbuf[c], axis=-1, keepdims=True) for c in range(n_slab)) * (1.0 / cw)
    var = sum(jnp.sum(jnp.square(vbuf[c] - mu), axis=-1, keepdims=True) for c in range(n_slab)) * (1.0 / cw)
    rstd = lax.rsqrt(var + EPS)
    y = jnp.zeros((tm, cw), F32)
    for c0 in range(0, cw, CONF_GROUP):
        cs = slice(c0, c0 + CONF_GROUP)
        v = jnp.concatenate([vbuf[c0 // LANES + n] for n in range(CONF_GROUP // LANES)], axis=-1)
        ln = (v - mu) * rstd * lg_ref[:, cs] + lb_ref[:, cs]
        t = _silu(ln) * gbuf[:, cs]
        y = y + jnp.dot(t.astype(BF16), wout_ref[cs, :], preferred_element_type=F32)
    o_ref[0] = x + gate_ref[0] * _rms(y, pog_ref[...])


def _odd_call(x, mod_l, png, pog, win, cw, cb, lg, lb, wout):
    b, s, d = x.shape
    tile = pl.BlockSpec((1, TM, d), lambda i, j: (i, j, 0))
    whole = lambda a: pl.BlockSpec(a.shape, lambda i, j: (0,) * a.ndim)
    modspec = lambda k: pl.BlockSpec((1, 1, d), lambda i, j: (i, 0, k))
    return pl.pallas_call(
        _odd_kernel,
        grid=(b, s // TM),
        in_specs=[tile, modspec(0), modspec(1), modspec(2), whole(png), whole(pog), whole(win),
                  whole(cw), whole(cb), whole(lg), whole(lb), whole(wout)],
        out_specs=tile,
        out_shape=jax.ShapeDtypeStruct((b, s, d), F32),
        scratch_shapes=[pltpu.VMEM((CONF_WIDTH // LANES, TM + CONF_HALO, LANES), F32),
                        pltpu.VMEM((CONF_WIDTH // LANES, TM, LANES), F32),
                        pltpu.VMEM((TM, CONF_WIDTH), F32)],
        compiler_params=pltpu.CompilerParams(dimension_semantics=("arbitrary", "arbitrary"),
                                             vmem_limit_bytes=VMEM_LIMIT),
    )(x, mod_l, mod_l, mod_l, png, pog, win, cw, cb, lg, lb, wout)


def _pack_even_w_in(w):
    d = w.shape[0]
    w = w.astype(BF16)
    a = w[:, :4 * SC_WIDTH]
    o = 4 * SC_WIDTH
    c_q = w[:, o:o + Q_LORA]
    c_kv = w[:, o + Q_LORA:o + Q_LORA + KV_LORA]
    k_r = w[:, o + Q_LORA + KV_LORA:o + Q_LORA + KV_LORA + QK_ROPE]
    b_g = w[:, o + Q_LORA + KV_LORA + QK_ROPE:]
    k_r = jnp.concatenate([jnp.zeros((d, ROPE_LO), w.dtype), k_r,
                           jnp.zeros((d, HEAD_PAD - ROPE_LO - QK_ROPE), w.dtype)], axis=1)
    return jnp.concatenate([a, b_g, c_q, c_kv, k_r], axis=1)


def _pad_heads(w, width):
    k = w.shape[0]
    w = w.astype(BF16).reshape(k, MLA_HEADS, width)
    w = jnp.pad(w, ((0, 0), (0, 0), (0, HEAD_PAD - width)))
    return w.reshape(k, MLA_HEADS * HEAD_PAD)


def kernel(x, c, positions, ada_w, ada_b, pre_norm_g, post_norm_g, even_w_in, even_sc_conv_w, even_sc_conv_b, even_q_norm_g, even_kv_norm_g, even_w_uq, even_w_ukv, even_w_out, odd_w_in, odd_conv_w, odd_conv_b, odd_ln_g, odd_ln_b, odd_w_out):
    b, s, d = x.shape
    assert d == D_MODEL and s % TM == 0 and s % TQ == 0 and TM % TK == 0 and b <= SUBLANES
    assert TQ % CHUNK == 0 and TK % CHUNK == 0 and TQ % TK == 0
    assert CONV_SEG % SUBLANES and CONV_SEG_TAIL % SUBLANES
    assert (TM % (SUBLANES * CONV_SEG)) % (SUBLANES * CONV_SEG_TAIL) == 0

    c_pad = jnp.pad(c, ((0, SUBLANES - b), (0, 0)))
    mod = _mod_call(c_pad, ada_w, ada_b.reshape(DEPTH, 1, 3 * d))

    inv_freq = 1.0 / (ROPE_THETA ** (jnp.arange(0, QK_ROPE, 2, dtype=F32) / QK_ROPE))
    freq_row = jnp.tile(jnp.concatenate([inv_freq, inv_freq]), ROPE_PACK).reshape(1, LANES)
    pos_packed = jnp.repeat(positions.reshape(b, s // ROPE_PACK, ROPE_PACK), QK_ROPE, axis=2)
    cos_t, sin_t = _rope_call(pos_packed, freq_row)

    row = lambda a: a.reshape(1, -1)
    for layer in range(DEPTH):
        mod_l = mod[layer, :b].reshape(b, 1, 3 * d)
        png, pog = row(pre_norm_g[layer]), row(post_norm_g[layer])
        i = layer // 2
        if layer % 2 == 0:
            w_ukv = even_w_ukv[i].reshape(KV_LORA, MLA_HEADS, QK_NOPE + V_HEAD)
            wuk = _pad_heads(w_ukv[:, :, :QK_NOPE].reshape(KV_LORA, -1), QK_NOPE)
            wuvt = w_ukv[:, :, QK_NOPE:].reshape(KV_LORA, MLA_WIDTH).T.astype(BF16)
            wuq = _pad_heads(even_w_uq[i], QK_NOPE + QK_ROPE)
            ya, bg, q, k, vt = _even_in_call(
                x, mod_l, png, _pack_even_w_in(even_w_in[i]), even_sc_conv_w[i],
                row(even_sc_conv_b[i]), row(even_q_norm_g[i]), row(even_kv_norm_g[i]),
                wuq, wuk, wuvt, cos_t, sin_t)
            x = _attn_out_call(q, k, vt, bg, x, ya, mod_l, pog, even_w_out[i].astype(BF16))
        else:
            x = _odd_call(x, mod_l, png, pog, odd_w_in[i].astype(BF16), odd_conv_w[i],
                          row(odd_conv_b[i]), row(odd_ln_g[i]), row(odd_ln_b[i]),
                          odd_w_out[i].astype(BF16))
    return x
```
